```python
import math
import jax, jax.numpy as jnp
from jax import lax
import numpy as np

D_MODEL = 4096
BATCH = 8
SEQ = 2048
DEPTH = 4
DEC_BATCH = 1
DEC_SEQ = 8192
PAST_LEN = 128

N_DIFF_HEADS = 8
DIFF_HEAD_DIM = 128
DIFF_WIDTH = N_DIFF_HEADS * 2 * DIFF_HEAD_DIM
Q_BLOCK = 128
REL_BUCKETS = 32
REL_MAX_DIST = 128
GLA_HEADS = 4
GLA_DK = 256
GLA_DV = 512
GLA_WIDTH = GLA_HEADS * GLA_DV
GLA_LOWRANK = 16
GLA_TAU = 16.0
GLA_CHUNK = 64
D_FF = ((8 * D_MODEL // 3 + 255) // 256) * 256
EPS = 1e-6
IN_COLS = (DIFF_WIDTH, DIFF_WIDTH, DIFF_WIDTH,
           GLA_HEADS * GLA_DK, GLA_HEADS * GLA_DK, GLA_WIDTH, GLA_WIDTH, 2 * GLA_LOWRANK,
           D_MODEL, D_MODEL)
D_IN = sum(IN_COLS)
N_MOD = 6

kernel_name = "hybrid_diffattn_gla_adaln_encoder"


def rmsnorm(x, gain):
    xf = x.astype(jnp.float32)
    y = xf * lax.rsqrt(jnp.mean(xf * xf, axis=-1, keepdims=True) + EPS)
    return (y * gain.astype(jnp.float32)).astype(x.dtype)


def t5_bucket(rel):
    half = REL_BUCKETS // 2
    ret = jnp.where(rel > 0, half, 0)
    n = jnp.abs(rel)
    max_exact = half // 2
    nf = jnp.maximum(n, 1).astype(jnp.float32)
    large = max_exact + (jnp.log(nf / max_exact) / math.log(REL_MAX_DIST / max_exact)
                         * (half - max_exact)).astype(jnp.int32)
    large = jnp.minimum(large, half - 1)
    return ret + jnp.where(n < max_exact, n, large)


def diff_attention(q, k, v, lam, rel_bias):
    B, S = q.shape[0], q.shape[1]
    nb = S // Q_BLOCK
    scale = DIFF_HEAD_DIM ** -0.5
    qb = q.reshape(B, nb, Q_BLOCK, N_DIFF_HEADS, 2, DIFF_HEAD_DIM).transpose(1, 0, 2, 3, 4, 5)
    kpos = jnp.arange(S, dtype=jnp.int32)

    def block(args):
        qi, start = args
        logits = jnp.einsum('bqhmd,bkhmd->bhmqk', qi, k,
                            preferred_element_type=jnp.float32) * scale
        qpos = start + jnp.arange(Q_BLOCK, dtype=jnp.int32)
        bias = rel_bias[t5_bucket(kpos[None, :] - qpos[:, None])]
        logits = logits + bias.astype(jnp.float32).transpose(2, 0, 1)[None, :, None]
        p = jax.nn.softmax(logits, axis=-1)
        a = p[:, :, 0] - lam * p[:, :, 1]
        return jnp.einsum('bhqk,bkhv->bqhv', a.astype(v.dtype), v)

    starts = jnp.arange(nb, dtype=jnp.int32) * Q_BLOCK
    out = lax.map(block, (qb, starts))
    return out.transpose(1, 0, 2, 3, 4).reshape(B, S, N_DIFF_HEADS, 2 * DIFF_HEAD_DIM)


def gla_direction(q, k, v, g, include_diag):
    B, S, H, DK = q.shape
    DV = v.shape[-1]
    nc = S // GLA_CHUNK

    def to_chunks(t):
        return t.reshape(B, nc, GLA_CHUNK, H, t.shape[-1]).transpose(1, 0, 3, 2, 4)

    mask = jnp.tril(jnp.ones((GLA_CHUNK, GLA_CHUNK), dtype=bool), k=0 if include_diag else -1)

    def step(state, inp):
        qc, kc, vc, gc = inp
        b = jnp.cumsum(gc.astype(jnp.float32), axis=2)
        b_last = b[:, :, -1:]
        q_t = qc * jnp.exp(b)
        k_t = kc * jnp.exp(-b)
        attn = jnp.where(mask, jnp.einsum('bhtd,bhsd->bhts', q_t, k_t), 0.0)
        o = jnp.einsum('bhts,bhsv->bhtv', attn, vc) + jnp.einsum('bhtd,bhdv->bhtv', q_t, state)
        k_s = kc * jnp.exp(b_last - b)
        state = state * jnp.exp(b_last[:, :, 0])[..., None] + jnp.einsum('bhsd,bhsv->bhdv', k_s, vc)
        return state, o

    state0 = jnp.zeros((B, H, DK, DV), jnp.float32)
    _, o = lax.scan(step, state0, (to_chunks(q), to_chunks(k), to_chunks(v), to_chunks(g)))
    return o.transpose(1, 0, 3, 2, 4).reshape(B, S, H, DV)


def head_rmsnorm(o, gain):
    of = o.astype(jnp.float32)
    return of * lax.rsqrt(jnp.mean(of * of, axis=-1, keepdims=True) + EPS) * gain.astype(jnp.float32)


def mixer(h, lam_init, rel_bias, w_in, lam_q1, lam_k1, lam_q2, lam_k2, diff_norm,
          w_gate2, b_gate2, gla_norm, w_br_a, w_br_b, w_out):
    B, S, _ = h.shape
    proj = h @ w_in
    offs = list(np.cumsum(IN_COLS)[:-1])
    qa, ka, va, qg, kg, vg, rg, lr, gate_a, gate_b = jnp.split(proj, offs, axis=-1)

    lam = (jnp.exp(jnp.sum(lam_q1.astype(jnp.float32) * lam_k1.astype(jnp.float32)))
           - jnp.exp(jnp.sum(lam_q2.astype(jnp.float32) * lam_k2.astype(jnp.float32))) + lam_init)
    oa = diff_attention(qa.reshape(B, S, N_DIFF_HEADS, 2, DIFF_HEAD_DIM),
                        ka.reshape(B, S, N_DIFF_HEADS, 2, DIFF_HEAD_DIM),
                        va.reshape(B, S, N_DIFF_HEADS, 2 * DIFF_HEAD_DIM), lam, rel_bias)
    oa = (head_rmsnorm(oa, diff_norm) * (1.0 - lam_init)).astype(h.dtype).reshape(B, S, DIFF_WIDTH)

    qg = qg.reshape(B, S, GLA_HEADS, GLA_DK) * (GLA_DK ** -0.5)
    kg = kg.reshape(B, S, GLA_HEADS, GLA_DK)
    vg = vg.reshape(B, S, GLA_HEADS, GLA_DV)
    lr = lr.reshape(B, S, 2, GLA_LOWRANK)
    glog = jax.nn.log_sigmoid((jnp.einsum('bsjr,jrd->bsjd', lr, w_gate2) + b_gate2).astype(jnp.float32)) / GLA_TAU
    glog = glog.reshape(B, S, 2, GLA_HEADS, GLA_DK)
    o_f = gla_direction(qg, kg, vg, glog[:, :, 0], True)
    o_b = jnp.flip(gla_direction(jnp.flip(qg, 1), jnp.flip(kg, 1), jnp.flip(vg, 1),
                                 jnp.flip(glog[:, :, 1], 1), False), 1)
    ob = head_rmsnorm(o_f + o_b, gla_norm).reshape(B, S, GLA_WIDTH)
    ob = (ob * jax.nn.silu(rg.astype(jnp.float32))).astype(h.dtype)

    merged = jax.nn.sigmoid(gate_a) * (oa @ w_br_a) + jax.nn.sigmoid(gate_b) * (ob @ w_br_b)
    return merged @ w_out


def swiglu(h, w_ffn_in, w_ffn_out):
    gu = h @ w_ffn_in
    g, u = jnp.split(gu, 2, axis=-1)
    return (jax.nn.silu(g) * u) @ w_ffn_out


def trunk(x, c, rel_bias, w_ada, b_ada, norm_mix, norm_ffn, w_in, lam_q1, lam_k1, lam_q2, lam_k2,
          diff_norm, w_gate2, b_gate2, gla_norm, w_br_a, w_br_b, w_out, w_ffn_in, w_ffn_out, norm_final):
    for l in range(DEPTH):
        lam_init = 0.8 - 0.6 * math.exp(-0.3 * l)
        mod = jax.nn.silu(c) @ w_ada[l] + b_ada[l]
        sh_m, sc_m, gt_m, sh_f, sc_f, gt_f = [m[:, None, :] for m in jnp.split(mod, N_MOD, axis=-1)]
        h = rmsnorm(x, norm_mix[l]) * (1 + sc_m) + sh_m
        x = x + gt_m * mixer(h, lam_init, rel_bias, w_in[l], lam_q1[l], lam_k1[l], lam_q2[l], lam_k2[l],
                             diff_norm[l], w_gate2[l], b_gate2[l], gla_norm[l], w_br_a[l], w_br_b[l], w_out[l])
        h = rmsnorm(x, norm_ffn[l]) * (1 + sc_f) + sh_f
        x = x + gt_f * swiglu(h, w_ffn_in[l], w_ffn_out[l])
    return rmsnorm(x, norm_final)


def setup_inputs(seed: int = 0) -> dict:
    key = jax.random.key(seed)
    ks = jax.random.split(key, 26)
    f32 = jnp.float32
    nrm = lambda k, shape, s: jax.random.normal(k, shape, f32) * s
    gain = lambda k, shape: 1.0 + 0.02 * jax.random.normal(k, shape, f32)
    return {
        "x_prompt": nrm(ks[0], (BATCH, SEQ, D_MODEL), 1.0),
        "x_sample": nrm(ks[1], (DEC_BATCH, DEC_SEQ, D_MODEL), 1.0),
        "c_prompt": nrm(ks[2], (BATCH, D_MODEL), 1.0),
        "c_sample": nrm(ks[3], (DEC_BATCH, D_MODEL), 1.0),
        "rel_bias": nrm(ks[4], (REL_BUCKETS, N_DIFF_HEADS), 0.5),
        "w_ada": nrm(ks[5], (DEPTH, D_MODEL, N_MOD * D_MODEL), 0.5 * D_MODEL ** -0.5),
        "b_ada": nrm(ks[6], (DEPTH, N_MOD * D_MODEL), 0.02),
        "norm_mix": gain(ks[7], (DEPTH, D_MODEL)),
        "norm_ffn": gain(ks[8], (DEPTH, D_MODEL)),
        "w_in": nrm(ks[9], (DEPTH, D_MODEL, D_IN), D_MODEL ** -0.5),
        "lam_q1": nrm(ks[10], (DEPTH, DIFF_HEAD_DIM), 0.1),
        "lam_k1": nrm(ks[11], (DEPTH, DIFF_HEAD_DIM), 0.1),
        "lam_q2": nrm(ks[12], (DEPTH, DIFF_HEAD_DIM), 0.1),
        "lam_k2": nrm(ks[13], (DEPTH, DIFF_HEAD_DIM), 0.1),
        "diff_norm": gain(ks[14], (DEPTH, 2 * DIFF_HEAD_DIM)),
        "w_gate2": nrm(ks[15], (DEPTH, 2, GLA_LOWRANK, GLA_HEADS * GLA_DK), GLA_LOWRANK ** -0.5),
        "b_gate2": nrm(ks[16], (DEPTH, 2, GLA_HEADS * GLA_DK), 0.1),
        "gla_norm": gain(ks[17], (DEPTH, GLA_DV)),
        "w_br_a": nrm(ks[18], (DEPTH, DIFF_WIDTH, D_MODEL), DIFF_WIDTH ** -0.5),
        "w_br_b": nrm(ks[19], (DEPTH, GLA_WIDTH, D_MODEL), GLA_WIDTH ** -0.5),
        "w_out": nrm(ks[20], (DEPTH, D_MODEL, D_MODEL), D_MODEL ** -0.5),
        "w_ffn_in": nrm(ks[21], (DEPTH, D_MODEL, 2 * D_FF), D_MODEL ** -0.5),
        "w_ffn_out": nrm(ks[22], (DEPTH, D_FF, D_MODEL), D_FF ** -0.5),
        "norm_final": gain(ks[23], (D_MODEL,)),
    }


def reference(x_prompt, x_sample, c_prompt, c_sample, rel_bias, w_ada, b_ada, norm_mix, norm_ffn, w_in,
              lam_q1, lam_k1, lam_q2, lam_k2, diff_norm, w_gate2, b_gate2, gla_norm, w_br_a, w_br_b,
              w_out, w_ffn_in, w_ffn_out, norm_final):
    y_prompt = trunk(x_prompt, c_prompt, rel_bias, w_ada, b_ada, norm_mix, norm_ffn, w_in, lam_q1, lam_k1,
                     lam_q2, lam_k2, diff_norm, w_gate2, b_gate2, gla_norm, w_br_a, w_br_b, w_out,
                     w_ffn_in, w_ffn_out, norm_final)
    y_sample = trunk(x_sample, c_sample, rel_bias, w_ada, b_ada, norm_mix, norm_ffn, w_in, lam_q1, lam_k1,
                     lam_q2, lam_k2, diff_norm, w_gate2, b_gate2, gla_norm, w_br_a, w_br_b, w_out,
                     w_ffn_in, w_ffn_out, norm_final)
    return (y_prompt, y_sample)
```

```python
import functools
import math

import jax
import jax.numpy as jnp
from jax import lax
from jax.experimental import pallas as pl
from jax.experimental.pallas import tpu as pltpu

N_DIFF_HEADS = 8
DIFF_HEAD_DIM = 128
REL_BUCKETS = 32
REL_MAX_DIST = 128
GLA_HEADS = 4
GLA_DK = 256
GLA_DV = 512
GLA_LOWRANK = 16
GLA_TAU = 16.0
GLA_CHUNK = 64
N_MOD = 6
EPS = 1e-6

LANES = 128
VMEM_LIMIT_BYTES = 56 * 1024 * 1024

ROW_TILE = 1024
NORM_ROWS = 256
ATTN_TILE = 512
GLA_BLOCK = 256
MOD_ROWS = 16

F32 = jnp.float32
BF16 = jnp.bfloat16


def _params(*sem):
    return pltpu.CompilerParams(dimension_semantics=sem, vmem_limit_bytes=VMEM_LIMIT_BYTES)


def _dot(a, b):
    return jnp.dot(a, b, preferred_element_type=F32)


def _dot_nt(a, b):
    return lax.dot_general(a, b, (((1,), (1,)), ((), ())), preferred_element_type=F32)


def _dot_tn(a, b):
    return lax.dot_general(a, b, (((0,), (0,)), ((), ())), preferred_element_type=F32)


def _sigmoid(x):
    return 1.0 / (1.0 + jnp.exp(-x))


def _mod_kernel(c_ref, w_ref, b_ref, o_ref):
    c = c_ref[...]
    a = (c * _sigmoid(c)).astype(BF16)
    o_ref[...] = _dot(a, w_ref[...].astype(BF16)) + b_ref[...]


def _modulation(c_rows, w_ada, b_ada):
    depth, d, n = w_ada.shape
    tn = 512
    return pl.pallas_call(
        _mod_kernel,
        out_shape=jax.ShapeDtypeStruct((depth, MOD_ROWS, n), F32),
        grid=(depth, n // tn),
        in_specs=[
            pl.BlockSpec((MOD_ROWS, d), lambda l, j: (0, 0)),
            pl.BlockSpec((None, d, tn), lambda l, j: (l, 0, j)),
            pl.BlockSpec((None, 1, tn), lambda l, j: (l, 0, j)),
        ],
        out_specs=pl.BlockSpec((None, MOD_ROWS, tn), lambda l, j: (l, 0, j)),
        compiler_params=_params("arbitrary", "arbitrary"),
        name="modulation",
    )(c_rows, w_ada, b_ada.reshape(depth, 1, n))


def _norm_mod_kernel(x_ref, g_ref, sc_ref, sh_ref, h_ref):
    x = x_ref[...]
    y = x * lax.rsqrt(jnp.mean(x * x, axis=-1, keepdims=True) + EPS) * g_ref[...]
    h_ref[...] = (y * (1.0 + sc_ref[...]) + sh_ref[...]).astype(BF16)


def _norm_mod_lr_kernel(x_ref, g_ref, sc_ref, sh_ref, wlr_ref, h_ref, lr_ref):
    x = x_ref[...]
    y = x * lax.rsqrt(jnp.mean(x * x, axis=-1, keepdims=True) + EPS) * g_ref[...]
    h = (y * (1.0 + sc_ref[...]) + sh_ref[...]).astype(BF16)
    h_ref[...] = h
    lr_ref[...] = _dot(h, wlr_ref[...])


def _norm_mod(x, gain, mod, shift_idx, scale_idx, seg_of_row_tile, w_lr=None):
    t, d = x.shape
    tm = NORM_ROWS
    seg = functools.partial(seg_of_row_tile, tm)
    in_specs = [
        pl.BlockSpec((tm, d), lambda i: (i, 0)),
        pl.BlockSpec((1, d), lambda i: (0, 0)),
        pl.BlockSpec((None, None, 1, d), lambda i: (seg(i), scale_idx, 0, 0)),
        pl.BlockSpec((None, None, 1, d), lambda i: (seg(i), shift_idx, 0, 0)),
    ]
    h_spec = pl.BlockSpec((tm, d), lambda i: (i, 0))
    h_shape = jax.ShapeDtypeStruct((t, d), BF16)
    if w_lr is None:
        return pl.pallas_call(
            _norm_mod_kernel, out_shape=h_shape, grid=(t // tm,), in_specs=in_specs, out_specs=h_spec,
            compiler_params=_params("arbitrary"), name="norm_mod",
        )(x, gain, mod, mod)
    return pl.pallas_call(
        _norm_mod_lr_kernel,
        out_shape=(h_shape, jax.ShapeDtypeStruct((t, LANES), F32)),
        grid=(t // tm,),
        in_specs=in_specs + [pl.BlockSpec((d, LANES), lambda i: (0, 0))],
        out_specs=(h_spec, pl.BlockSpec((tm, LANES), lambda i: (i, 0))),
        compiler_params=_params("arbitrary"), name="norm_mod_lr",
    )(x, gain, mod, mod, w_lr)


def _final_norm_kernel(x_ref, g_ref, o_ref):
    x = x_ref[...]
    o_ref[...] = x * lax.rsqrt(jnp.mean(x * x, axis=-1, keepdims=True) + EPS) * g_ref[...]


def _final_norm(x, gain):
    t, d = x.shape
    tm = NORM_ROWS
    return pl.pallas_call(
        _final_norm_kernel, out_shape=jax.ShapeDtypeStruct((t, d), F32), grid=(t // tm,),
        in_specs=[pl.BlockSpec((tm, d), lambda i: (i, 0)), pl.BlockSpec((1, d), lambda i: (0, 0))],
        out_specs=pl.BlockSpec((tm, d), lambda i: (i, 0)),
        compiler_params=_params("arbitrary"), name="final_norm",
    )(x, gain)


def _proj_kernel(a_ref, b_ref, o_ref):
    o_ref[...] = _dot(a_ref[...], b_ref[...]).astype(o_ref.dtype)


def _proj(h, w, tm, tn):
    t, k = h.shape
    n = w.shape[1]
    return pl.pallas_call(
        _proj_kernel, out_shape=jax.ShapeDtypeStruct((t, n), BF16), grid=(t // tm, n // tn),
        in_specs=[pl.BlockSpec((tm, k), lambda i, j: (i, 0)), pl.BlockSpec((k, tn), lambda i, j: (0, j))],
        out_specs=pl.BlockSpec((tm, tn), lambda i, j: (i, j)),
        compiler_params=_params("arbitrary", "arbitrary"), name="in_proj",
    )(h, w)


def _merge_kernel(oa_ref, ob_ref, wa_ref, wb_ref, ga_ref, gb_ref, o_ref):
    ya = _dot(oa_ref[...], wa_ref[...])
    yb = _dot(ob_ref[...], wb_ref[...])
    o_ref[...] = (_sigmoid(ga_ref[...].astype(F32)) * ya + _sigmoid(gb_ref[...].astype(F32)) * yb).astype(BF16)


def _merge(oa, ob, w_a, w_b, proj, gate_a_col, gate_b_col, tm, tn):
    t, ka = oa.shape
    kb = ob.shape[1]
    n = w_a.shape[1]
    ja, jb = gate_a_col // tn, gate_b_col // tn
    return pl.pallas_call(
        _merge_kernel, out_shape=jax.ShapeDtypeStruct((t, n), BF16), grid=(t // tm, n // tn),
        in_specs=[
            pl.BlockSpec((tm, ka), lambda i, j: (i, 0)),
            pl.BlockSpec((tm, kb), lambda i, j: (i, 0)),
            pl.BlockSpec((ka, tn), lambda i, j: (0, j)),
            pl.BlockSpec((kb, tn), lambda i, j: (0, j)),
            pl.BlockSpec((tm, tn), lambda i, j: (i, ja + j)),
            pl.BlockSpec((tm, tn), lambda i, j: (i, jb + j)),
        ],
        out_specs=pl.BlockSpec((tm, tn), lambda i, j: (i, j)),
        compiler_params=_params("arbitrary", "arbitrary"), name="merge",
    )(oa, ob, w_a, w_b, proj, proj)


def _resid_kernel(a_ref, b_ref, x_ref, gt_ref, o_ref):
    o_ref[...] = x_ref[...] + gt_ref[...] * _dot(a_ref[...], b_ref[...])


def _resid_matmul(a, w, x, mod, gate_idx, seg_of_row_tile, tm, tn, name):
    t, k = a.shape
    n = w.shape[1]
    seg = functools.partial(seg_of_row_tile, tm)
    return pl.pallas_call(
        _resid_kernel, out_shape=jax.ShapeDtypeStruct((t, n), F32), grid=(t // tm, n // tn),
        in_specs=[
            pl.BlockSpec((tm, k), lambda i, j: (i, 0)),
            pl.BlockSpec((k, tn), lambda i, j: (0, j)),
            pl.BlockSpec((tm, tn), lambda i, j: (i, j)),
            pl.BlockSpec((None, None, 1, tn), lambda i, j: (seg(i), gate_idx, 0, j)),
        ],
        out_specs=pl.BlockSpec((tm, tn), lambda i, j: (i, j)),
        compiler_params=_params("arbitrary", "arbitrary"), name=name,
    )(a, w, x, mod)


def _ffn_in_kernel(a_ref, wg_ref, wu_ref, o_ref):
    a = a_ref[...]
    g = _dot(a, wg_ref[...])
    u = _dot(a, wu_ref[...])
    o_ref[...] = (g * _sigmoid(g) * u).astype(BF16)


def _ffn_in(h, w, d_ff, tm, tn):
    t, k = h.shape
    nj = d_ff // tn
    return pl.pallas_call(
        _ffn_in_kernel, out_shape=jax.ShapeDtypeStruct((t, d_ff), BF16), grid=(t // tm, nj),
        in_specs=[
            pl.BlockSpec((tm, k), lambda i, j: (i, 0)),
            pl.BlockSpec((k, tn), lambda i, j: (0, j)),
            pl.BlockSpec((k, tn), lambda i, j: (0, nj + j)),
        ],
        out_specs=pl.BlockSpec((tm, tn), lambda i, j: (i, j)),
        compiler_params=_params("arbitrary", "arbitrary"), name="ffn_in",
    )(h, w, w)


def _t5_bucket(rel):
    half = REL_BUCKETS // 2
    ret = jnp.where(rel > 0, half, 0)
    n = jnp.abs(rel)
    max_exact = half // 2
    nf = jnp.maximum(n, 1).astype(jnp.float32)
    large = max_exact + (jnp.log(nf / max_exact) / math.log(REL_MAX_DIST / max_exact)
                         * (half - max_exact)).astype(jnp.int32)
    large = jnp.minimum(large, half - 1)
    return ret + jnp.where(n < max_exact, n, large)


def _rel_bias_rows(rel_bias, s):
    rel = jnp.arange(-(s - 1), s + 1, dtype=jnp.int32)
    rows = rel_bias.astype(F32)[_t5_bucket(rel)]
    return rows.T.reshape(rel_bias.shape[1], 1, 2 * s)


def _attn_kernel(lq1_ref, lk1_ref, lq2_ref, lk2_ref, tab_lo_ref, tab_hi_ref, q_ref, k_ref, v_ref, dn_ref,
                 _prev_out_ref, o_ref, m_sc, l_sc, acc_sc, *, lam_init, tile):
    ki = pl.program_id(3)
    dh = DIFF_HEAD_DIM

    @pl.when(ki == 0)
    def _():
        m_sc[...] = jnp.full(m_sc.shape, -jnp.inf, F32)
        l_sc[...] = jnp.zeros(l_sc.shape, F32)
        acc_sc[...] = jnp.zeros(acc_sc.shape, F32)

    window = jnp.concatenate([tab_lo_ref[...], tab_hi_ref[...]], axis=1)
    window = jnp.broadcast_to(window, (tile, 2 * tile))
    bias = pltpu.roll(window, tile + 1, 1, stride=1, stride_axis=0)[:, :tile]

    scale = dh ** -0.5
    v = v_ref[...]
    for m in range(2):
        q = q_ref[:, m * dh:(m + 1) * dh]
        k = k_ref[:, m * dh:(m + 1) * dh]
        s = _dot_nt(q, k) * scale + bias
        m_old = m_sc[m]
        m_new = jnp.maximum(m_old, jnp.max(s, axis=-1, keepdims=True))
        alpha = jnp.exp(m_old - m_new)
        p = jnp.exp(s - m_new)
        l_sc[m] = alpha * l_sc[m] + jnp.sum(p, axis=-1, keepdims=True)
        acc_sc[m] = alpha * acc_sc[m] + _dot(p.astype(BF16), v)
        m_sc[m] = m_new

    @pl.when(ki == pl.num_programs(3) - 1)
    def _():
        lam = (jnp.exp(jnp.sum(lq1_ref[...] * lk1_ref[...], axis=-1, keepdims=True))
               - jnp.exp(jnp.sum(lq2_ref[...] * lk2_ref[...], axis=-1, keepdims=True)) + lam_init)
        o = acc_sc[0] / l_sc[0] - lam * (acc_sc[1] / l_sc[1])
        o = o * lax.rsqrt(jnp.mean(o * o, axis=-1, keepdims=True) + EPS) * dn_ref[...] * (1.0 - lam_init)
        o_ref[...] = o.astype(BF16)


def _diff_attention(proj, out, row_off, batch, s, tab, lam_vecs, diff_norm, lam_init, col_q, col_k, col_v):
    tile = min(ATTN_TILE, s)
    nq = s // tile
    hw = 2 * DIFF_HEAD_DIM
    r0 = row_off // tile
    cq, ck, cv = col_q // hw, col_k // hw, col_v // hw
    vec_spec = pl.BlockSpec((1, DIFF_HEAD_DIM), lambda b, h, qi, ki: (0, 0))
    kernel = functools.partial(_attn_kernel, lam_init=lam_init, tile=tile)
    return pl.pallas_call(
        kernel,
        out_shape=jax.ShapeDtypeStruct(out.shape, out.dtype),
        grid=(batch, N_DIFF_HEADS, nq, nq),
        in_specs=[
            vec_spec, vec_spec, vec_spec, vec_spec,
            pl.BlockSpec((None, 1, tile), lambda b, h, qi, ki: (h, 0, ki - qi + nq - 1)),
            pl.BlockSpec((None, 1, tile), lambda b, h, qi, ki: (h, 0, ki - qi + nq)),
            pl.BlockSpec((tile, hw), lambda b, h, qi, ki: (r0 + b * nq + qi, cq + h)),
            pl.BlockSpec((tile, hw), lambda b, h, qi, ki: (r0 + b * nq + ki, ck + h)),
            pl.BlockSpec((tile, hw), lambda b, h, qi, ki: (r0 + b * nq + ki, cv + h)),
            pl.BlockSpec((1, hw), lambda b, h, qi, ki: (0, 0)),
            pl.BlockSpec(memory_space=pl.ANY),
        ],
        out_specs=pl.BlockSpec((tile, hw), lambda b, h, qi, ki: (r0 + b * nq + qi, h)),
        scratch_shapes=[
            pltpu.VMEM((2, tile, 1), F32),
            pltpu.VMEM((2, tile, 1), F32),
            pltpu.VMEM((2, tile, hw), F32),
        ],
        input_output_aliases={10: 0},
        compiler_params=_params("arbitrary", "arbitrary", "arbitrary", "arbitrary"),
        name="diff_attention",
    )(*lam_vecs, tab, tab, proj, proj, proj, diff_norm, out)


def _gla_kernel(*refs, reverse, block, final):
    if final:
        q_ref, k_ref, v_ref, lr_ref, wg_ref, bg_ref, r_ref, of_ref, gn_ref, _prev_out_ref, o_ref, state_sc = refs
    else:
        q_ref, k_ref, v_ref, lr_ref, wg_ref, bg_ref, _prev_out_ref, o_ref, state_sc = refs
    c = GLA_CHUNK
    shift = c.bit_length() - 1
    assert (1 << shift) == c and block % c == 0

    @pl.when(pl.program_id(2) == 0)
    def _():
        state_sc[...] = jnp.zeros(state_sc.shape, F32)

    x = _dot(lr_ref[...].astype(BF16), wg_ref[...]) + bg_ref[...]
    g = (jnp.minimum(x, 0.0) - jnp.log(1.0 + jnp.exp(-jnp.abs(x)))) / GLA_TAU

    row = lax.broadcasted_iota(jnp.int32, (block, block), 0)
    col = lax.broadcasted_iota(jnp.int32, (block, block), 1)
    same_chunk = lax.shift_right_logical(row, shift) == lax.shift_right_logical(col, shift)
    tri = jnp.where(same_chunk & ((col >= row) if reverse else (col <= row)), 1.0, 0.0).astype(BF16)
    g_hi = g.astype(BF16)
    rem = g - g_hi.astype(F32)
    g_mid = rem.astype(BF16)
    g_lo = (rem - g_mid.astype(F32)).astype(BF16)
    b = _dot(tri, g_hi) + _dot(tri, g_mid) + _dot(tri, g_lo)

    kf = k_ref[...].astype(F32)
    q_t_all = (q_ref[...].astype(F32) * (GLA_DK ** -0.5) * jnp.exp(b)).astype(BF16)
    k_t_all = (kf * jnp.exp(-b)).astype(BF16)

    ci = lax.broadcasted_iota(jnp.int32, (c, c), 0)
    cj = lax.broadcasted_iota(jnp.int32, (c, c), 1)
    keep = (cj > ci) if reverse else (cj <= ci)

    n_chunks = block // c
    for ch in (range(n_chunks - 1, -1, -1) if reverse else range(n_chunks)):
        lo, hi = ch * c, (ch + 1) * c
        b_c = b[lo:hi]
        b_last = b_c[0:1] if reverse else b_c[c - 1:c]
        q_t = q_t_all[lo:hi]
        v_c = v_ref[lo:hi, :]
        attn = jnp.where(keep, _dot_nt(q_t, k_t_all[lo:hi]), 0.0).astype(BF16)
        state = state_sc[...]
        o = _dot(attn, v_c) + _dot_nt(q_t, state.astype(BF16))
        k_s = (kf[lo:hi] * jnp.exp(b_last - b_c)).astype(BF16)
        state_sc[...] = state * jnp.exp(b_last) + _dot_tn(v_c, k_s)
        if final:
            o = o + of_ref[lo:hi, :]
            o = o * lax.rsqrt(jnp.mean(o * o, axis=-1, keepdims=True) + EPS) * gn_ref[...]
            r = r_ref[lo:hi, :].astype(F32)
            o_ref[lo:hi, :] = (o * (r * _sigmoid(r))).astype(BF16)
        else:
            o_ref[lo:hi, :] = o


def _gla_pass(proj, lr, w_gate, b_gate, out, row_off, batch, s, cols, *, reverse, o_fwd=None, gla_norm=None):
    final = o_fwd is not None
    block = min(GLA_BLOCK, s)
    nblk = s // block
    r0 = row_off // block
    col_q, col_k, col_v, col_r = cols

    def rows(b, t):
        return r0 + b * nblk + ((nblk - 1 - t) if reverse else t)

    in_specs = [
        pl.BlockSpec((block, GLA_DK), lambda b, h, t: (rows(b, t), col_q // GLA_DK + h)),
        pl.BlockSpec((block, GLA_DK), lambda b, h, t: (rows(b, t), col_k // GLA_DK + h)),
        pl.BlockSpec((block, GLA_DV), lambda b, h, t: (rows(b, t), col_v // GLA_DV + h)),
        pl.BlockSpec((block, LANES), lambda b, h, t: (rows(b, t), 0)),
        pl.BlockSpec((LANES, GLA_DK), lambda b, h, t: (0, h)),
        pl.BlockSpec((1, GLA_DK), lambda b, h, t: (0, h)),
    ]
    args = [proj, proj, proj, lr, w_gate, b_gate]
    if final:
        in_specs += [
            pl.BlockSpec((block, GLA_DV), lambda b, h, t: (rows(b, t), col_r // GLA_DV + h)),
            pl.BlockSpec((block, GLA_DV), lambda b, h, t: (rows(b, t), h)),
            pl.BlockSpec((1, GLA_DV), lambda b, h, t: (0, 0)),
        ]
        args += [proj, o_fwd, gla_norm]
    in_specs.append(pl.BlockSpec(memory_space=pl.ANY))
    args.append(out)
    return pl.pallas_call(
        functools.partial(_gla_kernel, reverse=reverse, block=block, final=final),
        out_shape=jax.ShapeDtypeStruct(out.shape, out.dtype),
        grid=(batch, GLA_HEADS, nblk),
        in_specs=in_specs,
        out_specs=pl.BlockSpec((block, GLA_DV), lambda b, h, t: (rows(b, t), h)),
        scratch_shapes=[pltpu.VMEM((GLA_DV, GLA_DK), F32)],
        input_output_aliases={len(args) - 1: 0},
        compiler_params=_params("arbitrary", "arbitrary", "arbitrary"),
        name="gla_bwd" if reverse else "gla_fwd",
    )(*args)


def _tile(n, pref):
    if n <= pref:
        return n
    t = (pref // LANES) * LANES
    while n % t:
        t -= LANES
    return t


def kernel(x_prompt, x_sample, c_prompt, c_sample, rel_bias, w_ada, b_ada, norm_mix, norm_ffn, w_in, lam_q1, lam_k1, lam_q2, lam_k2, diff_norm, w_gate2, b_gate2, gla_norm, w_br_a, w_br_b, w_out, w_ffn_in, w_ffn_out, norm_final):
    depth = w_in.shape[0]
    bp, sp, d = x_prompt.shape
    bs, ss, _ = x_sample.shape
    tp, ts = bp * sp, bs * ss
    d_ff = w_ffn_out.shape[1]
    diff_w = N_DIFF_HEADS * 2 * DIFF_HEAD_DIM
    gla_k, gla_w = GLA_HEADS * GLA_DK, GLA_HEADS * GLA_DV
    n_lr = 2 * GLA_LOWRANK
    assert bp + bs <= MOD_ROWS and n_lr <= LANES

    col_qa, col_ka, col_va = 0, diff_w, 2 * diff_w
    col_qg = 3 * diff_w
    col_kg = col_qg + gla_k
    col_vg = col_kg + gla_k
    col_rg = col_vg + gla_w
    col_lr = col_rg + gla_w
    col_ga = col_lr
    col_gb = col_ga + d
    n_main = col_gb + d

    tm = _tile(math.gcd(sp, ss), ROW_TILE)

    def seg_of_row_tile(rows_per_tile, i):
        r = i * rows_per_tile
        return jnp.where(r < tp, r // sp, bp + (r - tp) // ss)

    x = jnp.concatenate([x_prompt.reshape(tp, d), x_sample.reshape(ts, d)], axis=0)
    c_rows = jnp.zeros((MOD_ROWS, d), F32).at[:bp].set(c_prompt).at[bp:bp + bs].set(c_sample)
    mod = _modulation(c_rows, w_ada, b_ada).reshape(depth, MOD_ROWS, N_MOD, 1, d)

    tab_p = _rel_bias_rows(rel_bias, sp)
    tab_s = _rel_bias_rows(rel_bias, ss)
    gla_cols = (col_qg, col_kg, col_vg, col_rg)

    for l in range(depth):
        lam_init = 0.8 - 0.6 * math.exp(-0.3 * l)
        mod_l = mod[l]
        w_in_l = w_in[l]
        w_main = jnp.concatenate([w_in_l[:, :col_lr], w_in_l[:, col_lr + n_lr:]], axis=1).astype(BF16)
        w_lr = jnp.pad(w_in_l[:, col_lr:col_lr + n_lr], ((0, 0), (0, LANES - n_lr))).astype(BF16)
        wg_f = jnp.zeros((LANES, gla_k), F32).at[:GLA_LOWRANK].set(w_gate2[l, 0]).astype(BF16)
        wg_b = jnp.zeros((LANES, gla_k), F32).at[GLA_LOWRANK:n_lr].set(w_gate2[l, 1]).astype(BF16)
        bg_f = b_gate2[l, 0].reshape(1, gla_k)
        bg_b = b_gate2[l, 1].reshape(1, gla_k)
        lam_vecs = [v[l].reshape(1, DIFF_HEAD_DIM) for v in (lam_q1, lam_k1, lam_q2, lam_k2)]

        h, lr = _norm_mod(x, norm_mix[l].reshape(1, d), mod_l, 0, 1, seg_of_row_tile, w_lr=w_lr)
        proj = _proj(h, w_main, tm, _tile(n_main, 1024))

        oa = jnp.zeros((tp + ts, diff_w), BF16)
        dn = diff_norm[l].reshape(1, 2 * DIFF_HEAD_DIM)
        oa = _diff_attention(proj, oa, 0, bp, sp, tab_p, lam_vecs, dn, lam_init, col_qa, col_ka, col_va)
        oa = _diff_attention(proj, oa, tp, bs, ss, tab_s, lam_vecs, dn, lam_init, col_qa, col_ka, col_va)

        o_f = jnp.zeros((tp + ts, gla_w), F32)
        o_f = _gla_pass(proj, lr, wg_f, bg_f, o_f, 0, bp, sp, gla_cols, reverse=False)
        o_f = _gla_pass(proj, lr, wg_f, bg_f, o_f, tp, bs, ss, gla_cols, reverse=False)
        gn = gla_norm[l].reshape(1, GLA_DV)
        ob = jnp.zeros((tp + ts, gla_w), BF16)
        ob = _gla_pass(proj, lr, wg_b, bg_b, ob, 0, bp, sp, gla_cols, reverse=True, o_fwd=o_f, gla_norm=gn)
        ob = _gla_pass(proj, lr, wg_b, bg_b, ob, tp, bs, ss, gla_cols, reverse=True, o_fwd=o_f, gla_norm=gn)

        merged = _merge(oa, ob, w_br_a[l].astype(BF16), w_br_b[l].astype(BF16), proj, col_ga, col_gb,
                        tm, _tile(d, 512))
        x = _resid_matmul(merged, w_out[l].astype(BF16), x, mod_l, 2, seg_of_row_tile, tm, _tile(d, 512), "out_proj")

        h = _norm_mod(x, norm_ffn[l].reshape(1, d), mod_l, 3, 4, seg_of_row_tile)
        act = _ffn_in(h, w_ffn_in[l].astype(BF16), d_ff, tm, _tile(d_ff, 256))
        x = _resid_matmul(act, w_ffn_out[l].astype(BF16), x, mod_l, 5, seg_of_row_tile,
                          _tile(tm, 512), _tile(d, 512), "ffn_out")

    y = _final_norm(x, norm_final.reshape(1, d))
    return (y[:tp].reshape(bp, sp, d), y[tp:].reshape(bs, ss, d))
```

```python
import functools
import math

import jax
import jax.numpy as jnp
from jax import lax
from jax.experimental import pallas as pl
from jax.experimental.pallas import tpu as pltpu

N_DIFF_HEADS = 8
DIFF_HEAD_DIM = 128
REL_BUCKETS = 32
REL_MAX_DIST = 128
GLA_HEADS = 4
GLA_DK = 256
GLA_DV = 512
GLA_LOWRANK = 16
GLA_TAU = 16.0
GLA_CHUNK = 64
N_MOD = 6
EPS = 1e-6

LANES = 128
VMEM_LIMIT_BYTES = 56 * 1024 * 1024

ROW_TILE = 1024
NORM_ROWS = 256
ATTN_TILE = 512
ATTN_SUB = 128
ATTN_UNROLL = 4
N_BIAS_TILES = 5
GLA_BLOCK = 256
GLA_HEADS_PER_STEP = 4
LOG2E = 1.4426950408889634
MOD_ROWS = 16

F32 = jnp.float32
BF16 = jnp.bfloat16


def _params(*sem):
    return pltpu.CompilerParams(dimension_semantics=sem, vmem_limit_bytes=VMEM_LIMIT_BYTES)


def _dot(a, b):
    return jnp.dot(a, b, preferred_element_type=F32)


def _dot_nt(a, b):
    return lax.dot_general(a, b, (((1,), (1,)), ((), ())), preferred_element_type=F32)


def _dot_tn(a, b):
    return lax.dot_general(a, b, (((0,), (0,)), ((), ())), preferred_element_type=F32)


def _sigmoid(x):
    return 1.0 / (1.0 + jnp.exp(-x))


def _mod_kernel(c_ref, w_ref, b_ref, o_ref):
    @pl.when(pl.program_id(1) == 0)
    def _():
        o_ref[...] = jnp.broadcast_to(b_ref[...], o_ref.shape)

    c = c_ref[...]
    a = (c * _sigmoid(c)).astype(BF16)
    o_ref[...] += _dot(a, w_ref[...].astype(BF16))


def _modulation(c_rows, w_ada, b_ada):
    depth, d, n = w_ada.shape
    tk = LANES
    return pl.pallas_call(
        _mod_kernel,
        out_shape=jax.ShapeDtypeStruct((depth, MOD_ROWS, n), F32),
        grid=(depth, d // tk),
        in_specs=[
            pl.BlockSpec((MOD_ROWS, tk), lambda l, k: (0, k)),
            pl.BlockSpec((None, tk, n), lambda l, k: (l, k, 0)),
            pl.BlockSpec((None, 1, n), lambda l, k: (l, 0, 0)),
        ],
        out_specs=pl.BlockSpec((None, MOD_ROWS, n), lambda l, k: (l, 0, 0)),
        compiler_params=_params("arbitrary", "arbitrary"),
        name="modulation",
    )(c_rows, w_ada, b_ada.reshape(depth, 1, n))


def _norm_mod_kernel(x_ref, g_ref, sc_ref, sh_ref, h_ref):
    x = x_ref[...]
    y = x * lax.rsqrt(jnp.mean(x * x, axis=-1, keepdims=True) + EPS) * g_ref[...]
    h_ref[...] = (y * (1.0 + sc_ref[...]) + sh_ref[...]).astype(BF16)


def _norm_mod_lr_kernel(x_ref, g_ref, sc_ref, sh_ref, wlr_ref, h_ref, lr_ref):
    x = x_ref[...]
    y = x * lax.rsqrt(jnp.mean(x * x, axis=-1, keepdims=True) + EPS) * g_ref[...]
    h = (y * (1.0 + sc_ref[...]) + sh_ref[...]).astype(BF16)
    h_ref[...] = h
    lr_ref[...] = _dot(h, wlr_ref[...])


def _norm_mod(x, gain, mod, shift_idx, scale_idx, seg_of_row_tile, w_lr=None):
    t, d = x.shape
    tm = NORM_ROWS
    seg = functools.partial(seg_of_row_tile, tm)
    in_specs = [
        pl.BlockSpec((tm, d), lambda i: (i, 0)),
        pl.BlockSpec((1, d), lambda i: (0, 0)),
        pl.BlockSpec((None, None, 1, d), lambda i: (seg(i), scale_idx, 0, 0)),
        pl.BlockSpec((None, None, 1, d), lambda i: (seg(i), shift_idx, 0, 0)),
    ]
    h_spec = pl.BlockSpec((tm, d), lambda i: (i, 0))
    h_shape = jax.ShapeDtypeStruct((t, d), BF16)
    if w_lr is None:
        return pl.pallas_call(
            _norm_mod_kernel, out_shape=h_shape, grid=(t // tm,), in_specs=in_specs, out_specs=h_spec,
            compiler_params=_params("arbitrary"), name="norm_mod",
        )(x, gain, mod, mod)
    return pl.pallas_call(
        _norm_mod_lr_kernel,
        out_shape=(h_shape, jax.ShapeDtypeStruct((t, LANES), F32)),
        grid=(t // tm,),
        in_specs=in_specs + [pl.BlockSpec((d, LANES), lambda i: (0, 0))],
        out_specs=(h_spec, pl.BlockSpec((tm, LANES), lambda i: (i, 0))),
        compiler_params=_params("arbitrary"), name="norm_mod_lr",
    )(x, gain, mod, mod, w_lr)


def _final_norm_kernel(x_ref, g_ref, o_ref):
    x = x_ref[...]
    o_ref[...] = x * lax.rsqrt(jnp.mean(x * x, axis=-1, keepdims=True) + EPS) * g_ref[...]


def _final_norm(x, gain, row_off, rows):
    d = x.shape[1]
    tm = NORM_ROWS
    r0 = row_off // tm
    return pl.pallas_call(
        _final_norm_kernel, out_shape=jax.ShapeDtypeStruct((rows, d), F32), grid=(rows // tm,),
        in_specs=[pl.BlockSpec((tm, d), lambda i: (r0 + i, 0)), pl.BlockSpec((1, d), lambda i: (0, 0))],
        out_specs=pl.BlockSpec((tm, d), lambda i: (i, 0)),
        compiler_params=_params("arbitrary"), name="final_norm",
    )(x, gain)


def _proj_kernel(a_ref, b_ref, cs_ref, o_ref):
    o_ref[...] = (_dot(a_ref[...], b_ref[...]) * cs_ref[...]).astype(o_ref.dtype)


def _proj(h, w, col_scale, tm, tn):
    t, k = h.shape
    n = w.shape[1]
    return pl.pallas_call(
        _proj_kernel, out_shape=jax.ShapeDtypeStruct((t, n), BF16), grid=(t // tm, n // tn),
        in_specs=[pl.BlockSpec((tm, k), lambda i, j: (i, 0)), pl.BlockSpec((k, tn), lambda i, j: (0, j)),
                  pl.BlockSpec((1, tn), lambda i, j: (0, j))],
        out_specs=pl.BlockSpec((tm, tn), lambda i, j: (i, j)),
        compiler_params=_params("arbitrary", "arbitrary"), name="in_proj",
    )(h, w, col_scale)


def _merge_kernel(oa_ref, ob_ref, wa_ref, wb_ref, ga_ref, gb_ref, o_ref):
    ya = _dot(oa_ref[...], wa_ref[...])
    yb = _dot(ob_ref[...], wb_ref[...])
    o_ref[...] = (_sigmoid(ga_ref[...].astype(F32)) * ya + _sigmoid(gb_ref[...].astype(F32)) * yb).astype(BF16)


def _merge(oa, ob, w_a, w_b, proj, gate_a_col, gate_b_col, tm, tn):
    t, ka = oa.shape
    kb = ob.shape[1]
    n = w_a.shape[1]
    ja, jb = gate_a_col // tn, gate_b_col // tn
    return pl.pallas_call(
        _merge_kernel, out_shape=jax.ShapeDtypeStruct((t, n), BF16), grid=(t // tm, n // tn),
        in_specs=[
            pl.BlockSpec((tm, ka), lambda i, j: (i, 0)),
            pl.BlockSpec((tm, kb), lambda i, j: (i, 0)),
            pl.BlockSpec((ka, tn), lambda i, j: (0, j)),
            pl.BlockSpec((kb, tn), lambda i, j: (0, j)),
            pl.BlockSpec((tm, tn), lambda i, j: (i, ja + j)),
            pl.BlockSpec((tm, tn), lambda i, j: (i, jb + j)),
        ],
        out_specs=pl.BlockSpec((tm, tn), lambda i, j: (i, j)),
        compiler_params=_params("arbitrary", "arbitrary"), name="merge",
    )(oa, ob, w_a, w_b, proj, proj)


def _resid_kernel(a_ref, b_ref, x_ref, gt_ref, o_ref):
    o_ref[...] = x_ref[...] + gt_ref[...] * _dot(a_ref[...], b_ref[...])


def _resid_matmul(a, w, x, mod, gate_idx, seg_of_row_tile, tm, tn, name):
    t, k = a.shape
    n = w.shape[1]
    seg = functools.partial(seg_of_row_tile, tm)
    return pl.pallas_call(
        _resid_kernel, out_shape=jax.ShapeDtypeStruct((t, n), F32), grid=(t // tm, n // tn),
        in_specs=[
            pl.BlockSpec((tm, k), lambda i, j: (i, 0)),
            pl.BlockSpec((k, tn), lambda i, j: (0, j)),
            pl.BlockSpec((tm, tn), lambda i, j: (i, j)),
            pl.BlockSpec((None, None, 1, tn), lambda i, j: (seg(i), gate_idx, 0, j)),
        ],
        out_specs=pl.BlockSpec((tm, tn), lambda i, j: (i, j)),
        compiler_params=_params("arbitrary", "arbitrary"), name=name,
    )(a, w, x, mod)


def _ffn_in_kernel(a_ref, wg_ref, wu_ref, o_ref):
    a = a_ref[...]
    g = _dot(a, wg_ref[...])
    u = _dot(a, wu_ref[...])
    o_ref[...] = (g * _sigmoid(g) * u).astype(BF16)


def _ffn_in(h, w, d_ff, tm, tn):
    t, k = h.shape
    nj = d_ff // tn
    return pl.pallas_call(
        _ffn_in_kernel, out_shape=jax.ShapeDtypeStruct((t, d_ff), BF16), grid=(t // tm, nj),
        in_specs=[
            pl.BlockSpec((tm, k), lambda i, j: (i, 0)),
            pl.BlockSpec((k, tn), lambda i, j: (0, j)),
            pl.BlockSpec((k, tn), lambda i, j: (0, nj + j)),
        ],
        out_specs=pl.BlockSpec((tm, tn), lambda i, j: (i, j)),
        compiler_params=_params("arbitrary", "arbitrary"), name="ffn_in",
    )(h, w, w)


def _t5_bucket(rel):
    half = REL_BUCKETS // 2
    ret = jnp.where(rel > 0, half, 0)
    n = jnp.abs(rel)
    max_exact = half // 2
    nf = jnp.maximum(n, 1).astype(jnp.float32)
    large = max_exact + (jnp.log(nf / max_exact) / math.log(REL_MAX_DIST / max_exact)
                         * (half - max_exact)).astype(jnp.int32)
    large = jnp.minimum(large, half - 1)
    return ret + jnp.where(n < max_exact, n, large)


def _rel_bias_windows(rel_bias, tile):
    o = jnp.arange(N_BIAS_TILES, dtype=jnp.int32)[:, None]
    e = jnp.arange(2 * tile, dtype=jnp.int32)[None, :]
    rel = (o - N_BIAS_TILES // 2) * tile - (tile - 1) + e
    rows = rel_bias.astype(F32)[_t5_bucket(rel)] * LOG2E
    return jnp.transpose(rows, (2, 0, 1)).reshape(rel_bias.shape[1], N_BIAS_TILES, 1, 2 * tile)


def _bias_tile_kernel(win_ref, o_ref, *, tile):
    window = jnp.broadcast_to(win_ref[...], (tile, 2 * tile))
    o_ref[...] = pltpu.roll(window, tile + 1, 1, stride=1, stride_axis=0)[:, :tile]


def _bias_tiles(rel_bias, tile):
    assert tile >= REL_MAX_DIST
    heads = rel_bias.shape[1]
    win = _rel_bias_windows(rel_bias, tile)
    return pl.pallas_call(
        functools.partial(_bias_tile_kernel, tile=tile),
        out_shape=jax.ShapeDtypeStruct((heads, N_BIAS_TILES, tile, tile), F32),
        grid=(heads, N_BIAS_TILES),
        in_specs=[pl.BlockSpec((None, None, 1, 2 * tile), lambda h, o: (h, o, 0, 0))],
        out_specs=pl.BlockSpec((None, None, tile, tile), lambda h, o: (h, o, 0, 0)),
        compiler_params=_params("arbitrary", "arbitrary"), name="bias_tiles",
    )(win)


def _attn_kernel(lq1_ref, lk1_ref, lq2_ref, lk2_ref, bias_ref, q_ref, k_ref, v_ref, dn_ref,
                 _prev_out_ref, o_ref, m_sc, l_sc, acc_sc, *, lam_init, tile, n_key_tiles, unroll):
    qi = pl.program_id(2)
    dh = DIFF_HEAD_DIM
    hw = 2 * dh
    half = N_BIAS_TILES // 2
    m_sc[...] = jnp.full(m_sc.shape, -jnp.inf, F32)
    l_sc[...] = jnp.zeros(l_sc.shape, F32)
    acc_sc[...] = jnp.zeros(acc_sc.shape, F32)

    def body(kk, carry):
        key_tiles = [kk * unroll + u for u in range(unroll)]
        offs = [pl.multiple_of(kt * tile, tile) for kt in key_tiles]
        bsel = [jnp.clip(kt - qi, -half, half) + half for kt in key_tiles]
        vs = [v_ref[pl.ds(off, tile), :] for off in offs]
        for m in range(2):
            ks = [k_ref[pl.ds(off, tile), m * dh:(m + 1) * dh] for off in offs]
            for r in range(0, tile, ATTN_SUB):
                rows = slice(r, r + ATTN_SUB)
                q = q_ref[rows, m * dh:(m + 1) * dh]
                ss = [_dot_nt(q, ks[u]) + bias_ref[bsel[u], rows, :] for u in range(unroll)]
                m_old = m_sc[m, rows, :]
                m_new = m_old
                for s in ss:
                    m_new = jnp.maximum(m_new, jnp.max(s, axis=-1, keepdims=True))
                alpha = jnp.exp2(m_old - m_new)
                m_rep = pltpu.repeat(m_new, tile // LANES, 1)
                l_new = alpha * l_sc[m, rows, :]
                acc = pltpu.repeat(alpha, hw // LANES, 1) * acc_sc[m, rows, :]
                for u in range(unroll):
                    p = jnp.exp2(ss[u] - m_rep)
                    l_new = l_new + jnp.sum(p, axis=-1, keepdims=True)
                    acc = acc + _dot(p.astype(BF16), vs[u])
                l_sc[m, rows, :] = l_new
                acc_sc[m, rows, :] = acc
                m_sc[m, rows, :] = m_new
        return carry

    lax.fori_loop(0, n_key_tiles // unroll, body, 0)

    lam = (jnp.exp(jnp.sum(lq1_ref[...] * lk1_ref[...], axis=-1, keepdims=True))
           - jnp.exp(jnp.sum(lq2_ref[...] * lk2_ref[...], axis=-1, keepdims=True)) + lam_init)
    o = (acc_sc[0] / pltpu.repeat(l_sc[0], hw // LANES, 1)
         - lam * (acc_sc[1] / pltpu.repeat(l_sc[1], hw // LANES, 1)))
    o = o * lax.rsqrt(jnp.mean(o * o, axis=-1, keepdims=True) + EPS) * dn_ref[...] * (1.0 - lam_init)
    o_ref[...] = o.astype(BF16)


def _diff_attention(proj, out, row_off, batch, s, bias_tiles, lam_vecs, diff_norm, lam_init, col_q, col_k, col_v):
    tile = bias_tiles.shape[-1]
    nq = s // tile
    unroll = math.gcd(nq, ATTN_UNROLL)
    hw = 2 * DIFF_HEAD_DIM
    r0 = row_off // tile
    seq0 = row_off // s
    cq, ck, cv = col_q // hw, col_k // hw, col_v // hw
    vec_spec = pl.BlockSpec((1, DIFF_HEAD_DIM), lambda b, h, qi: (0, 0))
    kernel = functools.partial(_attn_kernel, lam_init=lam_init, tile=tile, n_key_tiles=nq, unroll=unroll)
    return pl.pallas_call(
        kernel,
        out_shape=jax.ShapeDtypeStruct(out.shape, out.dtype),
        grid=(batch, N_DIFF_HEADS, nq),
        in_specs=[
            vec_spec, vec_spec, vec_spec, vec_spec,
            pl.BlockSpec((None, N_BIAS_TILES, tile, tile), lambda b, h, qi: (h, 0, 0, 0)),
            pl.BlockSpec((tile, hw), lambda b, h, qi: (r0 + b * nq + qi, cq + h)),
            pl.BlockSpec((s, hw), lambda b, h, qi: (seq0 + b, ck + h)),
            pl.BlockSpec((s, hw), lambda b, h, qi: (seq0 + b, cv + h)),
            pl.BlockSpec((1, hw), lambda b, h, qi: (0, 0)),
            pl.BlockSpec(memory_space=pl.ANY),
        ],
        out_specs=pl.BlockSpec((tile, hw), lambda b, h, qi: (r0 + b * nq + qi, h)),
        scratch_shapes=[
            pltpu.VMEM((2, tile, LANES), F32),
            pltpu.VMEM((2, tile, LANES), F32),
            pltpu.VMEM((2, tile, hw), F32),
        ],
        input_output_aliases={9: 0},
        compiler_params=_params("arbitrary", "arbitrary", "arbitrary"),
        name="diff_attention",
    )(*lam_vecs, bias_tiles, proj, proj, proj, diff_norm, out)


def _gla_kernel(*refs, reverse, block, final, heads):
    if final:
        q_ref, k_ref, v_ref, lr_ref, wg_ref, bg_ref, r_ref, of_ref, gn_ref, _prev_out_ref, o_ref, state_sc = refs
    else:
        q_ref, k_ref, v_ref, lr_ref, wg_ref, bg_ref, _prev_out_ref, o_ref, state_sc = refs
    c = GLA_CHUNK
    shift = c.bit_length() - 1
    assert (1 << shift) == c and block % c == 0

    @pl.when(pl.program_id(2) == 0)
    def _():
        state_sc[...] = jnp.zeros(state_sc.shape, F32)

    x = _dot(lr_ref[...].astype(BF16), wg_ref[...]) + bg_ref[...]
    g = (jnp.minimum(x, 0.0) - jnp.log(1.0 + jnp.exp(-jnp.abs(x)))) / GLA_TAU

    row = lax.broadcasted_iota(jnp.int32, (block, block), 0)
    col = lax.broadcasted_iota(jnp.int32, (block, block), 1)
    same_chunk = lax.shift_right_logical(row, shift) == lax.shift_right_logical(col, shift)
    tri = jnp.where(same_chunk & ((col >= row) if reverse else (col <= row)), 1.0, 0.0).astype(BF16)
    g_hi = g.astype(BF16)
    rem = g - g_hi.astype(F32)
    g_mid = rem.astype(BF16)
    g_lo = (rem - g_mid.astype(F32)).astype(BF16)
    b = _dot(tri, g_hi) + _dot(tri, g_mid) + _dot(tri, g_lo)

    kf = k_ref[...].astype(F32)
    q_t_all = (q_ref[...].astype(F32) * jnp.exp(b)).astype(BF16)
    k_t_all = (kf * jnp.exp(-b)).astype(BF16)

    ci = lax.broadcasted_iota(jnp.int32, (c, c), 0)
    cj = lax.broadcasted_iota(jnp.int32, (c, c), 1)
    keep = (cj > ci) if reverse else (cj <= ci)

    n_chunks = block // c
    for ch in (range(n_chunks - 1, -1, -1) if reverse else range(n_chunks)):
        lo, hi = ch * c, (ch + 1) * c
        b_c = b[lo:hi]
        b_last = b_c[0:1] if reverse else b_c[c - 1:c]
        k_s_all = (kf[lo:hi] * jnp.exp(b_last - b_c)).astype(BF16)
        decay = jnp.exp(b_last)
        for hd in range(heads):
            kc = slice(hd * GLA_DK, (hd + 1) * GLA_DK)
            vc = slice(hd * GLA_DV, (hd + 1) * GLA_DV)
            q_t = q_t_all[lo:hi, kc]
            v_c = v_ref[lo:hi, vc]
            attn = jnp.where(keep, _dot_nt(q_t, k_t_all[lo:hi, kc]), 0.0).astype(BF16)
            state = state_sc[hd]
            o = _dot(attn, v_c) + _dot_nt(q_t, state.astype(BF16))
            state_sc[hd] = state * decay[:, kc] + _dot_tn(v_c, k_s_all[:, kc])
            if final:
                o = o + of_ref[lo:hi, vc]
                o = o * lax.rsqrt(jnp.mean(o * o, axis=-1, keepdims=True) + EPS) * gn_ref[...]
                r = r_ref[lo:hi, vc].astype(F32)
                o_ref[lo:hi, vc] = (o * (r * _sigmoid(r))).astype(BF16)
            else:
                o_ref[lo:hi, vc] = o


def _gla_pass(proj, lr, w_gate, b_gate, out, row_off, batch, s, cols, *, reverse, o_fwd=None, gla_norm=None):
    final = o_fwd is not None
    block = min(GLA_BLOCK, s)
    nblk = s // block
    r0 = row_off // block
    col_q, col_k, col_v, col_r = cols
    heads = math.gcd(GLA_HEADS, GLA_HEADS_PER_STEP)
    wk, wv = heads * GLA_DK, heads * GLA_DV
    assert col_q % wk == 0 and col_k % wk == 0 and col_v % wv == 0 and col_r % wv == 0

    def rows(b, t):
        return r0 + b * nblk + ((nblk - 1 - t) if reverse else t)

    in_specs = [
        pl.BlockSpec((block, wk), lambda b, h, t: (rows(b, t), col_q // wk + h)),
        pl.BlockSpec((block, wk), lambda b, h, t: (rows(b, t), col_k // wk + h)),
        pl.BlockSpec((block, wv), lambda b, h, t: (rows(b, t), col_v // wv + h)),
        pl.BlockSpec((block, LANES), lambda b, h, t: (rows(b, t), 0)),
        pl.BlockSpec((LANES, wk), lambda b, h, t: (0, h)),
        pl.BlockSpec((1, wk), lambda b, h, t: (0, h)),
    ]
    args = [proj, proj, proj, lr, w_gate, b_gate]
    if final:
        in_specs += [
            pl.BlockSpec((block, wv), lambda b, h, t: (rows(b, t), col_r // wv + h)),
            pl.BlockSpec((block, wv), lambda b, h, t: (rows(b, t), h)),
            pl.BlockSpec((1, GLA_DV), lambda b, h, t: (0, 0)),
        ]
        args += [proj, o_fwd, gla_norm]
    in_specs.append(pl.BlockSpec(memory_space=pl.ANY))
    args.append(out)
    return pl.pallas_call(
        functools.partial(_gla_kernel, reverse=reverse, block=block, final=final, heads=heads),
        out_shape=jax.ShapeDtypeStruct(out.shape, out.dtype),
        grid=(batch, GLA_HEADS // heads, nblk),
        in_specs=in_specs,
        out_specs=pl.BlockSpec((block, wv), lambda b, h, t: (rows(b, t), h)),
        scratch_shapes=[pltpu.VMEM((heads, GLA_DV, GLA_DK), F32)],
        input_output_aliases={len(args) - 1: 0},
        compiler_params=_params("arbitrary", "arbitrary", "arbitrary"),
        name="gla_bwd" if reverse else "gla_fwd",
    )(*args)


def _tile(n, pref):
    if n <= pref:
        return n
    t = (pref // LANES) * LANES
    while n % t:
        t -= LANES
    return t


def kernel(x_prompt, x_sample, c_prompt, c_sample, rel_bias, w_ada, b_ada, norm_mix, norm_ffn, w_in, lam_q1, lam_k1, lam_q2, lam_k2, diff_norm, w_gate2, b_gate2, gla_norm, w_br_a, w_br_b, w_out, w_ffn_in, w_ffn_out, norm_final):
    depth = w_in.shape[0]
    bp, sp, d = x_prompt.shape
    bs, ss, _ = x_sample.shape
    tp, ts = bp * sp, bs * ss
    d_ff = w_ffn_out.shape[1]
    diff_w = N_DIFF_HEADS * 2 * DIFF_HEAD_DIM
    gla_k, gla_w = GLA_HEADS * GLA_DK, GLA_HEADS * GLA_DV
    n_lr = 2 * GLA_LOWRANK
    assert bp + bs <= MOD_ROWS and n_lr <= LANES

    col_qa, col_ka, col_va = 0, diff_w, 2 * diff_w
    col_qg = 3 * diff_w
    col_kg = col_qg + gla_k
    col_vg = col_kg + gla_k
    col_rg = col_vg + gla_w
    col_lr = col_rg + gla_w
    col_ga = col_lr
    col_gb = col_ga + d
    n_main = col_gb + d

    tm = _tile(math.gcd(sp, ss), ROW_TILE)

    def seg_of_row_tile(rows_per_tile, i):
        r = i * rows_per_tile
        return jnp.where(r < tp, r // sp, bp + (r - tp) // ss)

    x = jnp.concatenate([x_prompt.reshape(tp, d), x_sample.reshape(ts, d)], axis=0)
    c_rows = jnp.zeros((MOD_ROWS, d), F32).at[:bp].set(c_prompt).at[bp:bp + bs].set(c_sample)
    mod = _modulation(c_rows, w_ada, b_ada).reshape(depth, MOD_ROWS, N_MOD, 1, d)

    attn_tile = _tile(math.gcd(sp, ss), ATTN_TILE)
    assert tp % ss == 0 and attn_tile % ATTN_SUB == 0
    bias_tiles = _bias_tiles(rel_bias, attn_tile)
    gla_cols = (col_qg, col_kg, col_vg, col_rg)
    col_scale = jnp.ones((1, n_main), F32)
    col_scale = col_scale.at[:, col_qa:col_ka].set(DIFF_HEAD_DIM ** -0.5 * LOG2E)
    col_scale = col_scale.at[:, col_qg:col_kg].set(GLA_DK ** -0.5)

    for l in range(depth):
        lam_init = 0.8 - 0.6 * math.exp(-0.3 * l)
        mod_l = mod[l]
        w_in_l = w_in[l]
        w_main = jnp.concatenate([w_in_l[:, :col_lr], w_in_l[:, col_lr + n_lr:]], axis=1).astype(BF16)
        w_lr = jnp.pad(w_in_l[:, col_lr:col_lr + n_lr], ((0, 0), (0, LANES - n_lr))).astype(BF16)
        wg_f = jnp.zeros((LANES, gla_k), F32).at[:GLA_LOWRANK].set(w_gate2[l, 0]).astype(BF16)
        wg_b = jnp.zeros((LANES, gla_k), F32).at[GLA_LOWRANK:n_lr].set(w_gate2[l, 1]).astype(BF16)
        bg_f = b_gate2[l, 0].reshape(1, gla_k)
        bg_b = b_gate2[l, 1].reshape(1, gla_k)
        lam_vecs = [v[l].reshape(1, DIFF_HEAD_DIM) for v in (lam_q1, lam_k1, lam_q2, lam_k2)]

        h, lr = _norm_mod(x, norm_mix[l].reshape(1, d), mod_l, 0, 1, seg_of_row_tile, w_lr=w_lr)
        proj = _proj(h, w_main, col_scale, tm, _tile(n_main, 1024))

        oa = jnp.zeros((tp + ts, diff_w), BF16)
        dn = diff_norm[l].reshape(1, 2 * DIFF_HEAD_DIM)
        oa = _diff_attention(proj, oa, 0, bp, sp, bias_tiles, lam_vecs, dn, lam_init, col_qa, col_ka, col_va)
        oa = _diff_attention(proj, oa, tp, bs, ss, bias_tiles, lam_vecs, dn, lam_init, col_qa, col_ka, col_va)

        o_f = jnp.zeros((tp + ts, gla_w), F32)
        o_f = _gla_pass(proj, lr, wg_f, bg_f, o_f, 0, bp, sp, gla_cols, reverse=False)
        o_f = _gla_pass(proj, lr, wg_f, bg_f, o_f, tp, bs, ss, gla_cols, reverse=False)
        gn = gla_norm[l].reshape(1, GLA_DV)
        ob = jnp.zeros((tp + ts, gla_w), BF16)
        ob = _gla_pass(proj, lr, wg_b, bg_b, ob, 0, bp, sp, gla_cols, reverse=True, o_fwd=o_f, gla_norm=gn)
        ob = _gla_pass(proj, lr, wg_b, bg_b, ob, tp, bs, ss, gla_cols, reverse=True, o_fwd=o_f, gla_norm=gn)

        merged = _merge(oa, ob, w_br_a[l].astype(BF16), w_br_b[l].astype(BF16), proj, col_ga, col_gb,
                        tm, _tile(d, 512))
        x = _resid_matmul(merged, w_out[l].astype(BF16), x, mod_l, 2, seg_of_row_tile, tm, _tile(d, 512), "out_proj")

        h = _norm_mod(x, norm_ffn[l].reshape(1, d), mod_l, 3, 4, seg_of_row_tile)
        act = _ffn_in(h, w_ffn_in[l].astype(BF16), d_ff, tm, _tile(d_ff, 256))
        x = _resid_matmul(act, w_ffn_out[l].astype(BF16), x, mod_l, 5, seg_of_row_tile,
                          _tile(tm, 512), _tile(d, 512), "ffn_out")

    gain = norm_final.reshape(1, d)
    return (_final_norm(x, gain, 0, tp).reshape(bp, sp, d), _final_norm(x, gain, tp, ts).reshape(bs, ss, d))
```

```python
import functools
import math

import jax
import jax.numpy as jnp
from jax import lax
from jax.experimental import pallas as pl
from jax.experimental.pallas import tpu as pltpu

N_DIFF_HEADS = 8
DIFF_HEAD_DIM = 128
REL_BUCKETS = 32
REL_MAX_DIST = 128
GLA_HEADS = 4
GLA_DK = 256
GLA_DV = 512
GLA_LOWRANK = 16
GLA_TAU = 16.0
GLA_CHUNK = 64
N_MOD = 6
EPS = 1e-6

LANES = 128
VMEM_LIMIT_BYTES = 56 * 1024 * 1024

ROW_TILE = 1024
FFN_IN_ROW_TILE = 2048
NORM_ROWS = 256
ATTN_TILE = 512
ATTN_SUB = 128
ATTN_UNROLL = 4
N_BIAS_TILES = 5
GLA_BLOCK = 256
GLA_HEADS_PER_STEP = 4
LOG2E = 1.4426950408889634
MOD_ROWS = 16

F32 = jnp.float32
BF16 = jnp.bfloat16


def _params(*sem):
    return pltpu.CompilerParams(dimension_semantics=sem, vmem_limit_bytes=VMEM_LIMIT_BYTES)


def _dot(a, b):
    return jnp.dot(a, b, preferred_element_type=F32)


def _dot_nt(a, b):
    return lax.dot_general(a, b, (((1,), (1,)), ((), ())), preferred_element_type=F32)


def _dot_tn(a, b):
    return lax.dot_general(a, b, (((0,), (0,)), ((), ())), preferred_element_type=F32)


def _sigmoid(x):
    return 0.5 * jnp.tanh(0.5 * x) + 0.5


def _lane_repeat(x, n):
    return jnp.concatenate([x] * n, axis=1) if n > 1 else x


def _mod_kernel(c_ref, w_ref, b_ref, o_ref):
    @pl.when(pl.program_id(1) == 0)
    def _():
        o_ref[...] = jnp.broadcast_to(b_ref[...], o_ref.shape)

    c = c_ref[...]
    a = (c * _sigmoid(c)).astype(BF16)
    o_ref[...] += _dot(a, w_ref[...].astype(BF16))


def _modulation(c_rows, w_ada, b_ada):
    depth, d, n = w_ada.shape
    tk = LANES
    return pl.pallas_call(
        _mod_kernel,
        out_shape=jax.ShapeDtypeStruct((depth, MOD_ROWS, n), F32),
        grid=(depth, d // tk),
        in_specs=[
            pl.BlockSpec((MOD_ROWS, tk), lambda l, k: (0, k)),
            pl.BlockSpec((None, tk, n), lambda l, k: (l, k, 0)),
            pl.BlockSpec((None, 1, n), lambda l, k: (l, 0, 0)),
        ],
        out_specs=pl.BlockSpec((None, MOD_ROWS, n), lambda l, k: (l, 0, 0)),
        compiler_params=_params("arbitrary", "arbitrary"),
        name="modulation",
    )(c_rows, w_ada, b_ada.reshape(depth, 1, n))


def _norm_mod_kernel(x_ref, g_ref, sc_ref, sh_ref, h_ref):
    x = x_ref[...]
    y = x * lax.rsqrt(jnp.mean(x * x, axis=-1, keepdims=True) + EPS) * g_ref[...]
    h_ref[...] = (y * (1.0 + sc_ref[...]) + sh_ref[...]).astype(BF16)


def _norm_mod_lr_kernel(x_ref, g_ref, sc_ref, sh_ref, wlr_ref, h_ref, lr_ref):
    x = x_ref[...]
    y = x * lax.rsqrt(jnp.mean(x * x, axis=-1, keepdims=True) + EPS) * g_ref[...]
    h = (y * (1.0 + sc_ref[...]) + sh_ref[...]).astype(BF16)
    h_ref[...] = h
    lr_ref[...] = _dot(h, wlr_ref[...])


def _norm_mod(x, gain, mod, shift_idx, scale_idx, seg_of_row_tile, w_lr=None):
    t, d = x.shape
    tm = NORM_ROWS
    seg = functools.partial(seg_of_row_tile, tm)
    in_specs = [
        pl.BlockSpec((tm, d), lambda i: (i, 0)),
        pl.BlockSpec((1, d), lambda i: (0, 0)),
        pl.BlockSpec((None, None, 1, d), lambda i: (seg(i), scale_idx, 0, 0)),
        pl.BlockSpec((None, None, 1, d), lambda i: (seg(i), shift_idx, 0, 0)),
    ]
    h_spec = pl.BlockSpec((tm, d), lambda i: (i, 0))
    h_shape = jax.ShapeDtypeStruct((t, d), BF16)
    if w_lr is None:
        return pl.pallas_call(
            _norm_mod_kernel, out_shape=h_shape, grid=(t // tm,), in_specs=in_specs, out_specs=h_spec,
            compiler_params=_params("arbitrary"), name="norm_mod",
        )(x, gain, mod, mod)
    return pl.pallas_call(
        _norm_mod_lr_kernel,
        out_shape=(h_shape, jax.ShapeDtypeStruct((t, LANES), F32)),
        grid=(t // tm,),
        in_specs=in_specs + [pl.BlockSpec((d, LANES), lambda i: (0, 0))],
        out_specs=(h_spec, pl.BlockSpec((tm, LANES), lambda i: (i, 0))),
        compiler_params=_params("arbitrary"), name="norm_mod_lr",
    )(x, gain, mod, mod, w_lr)


def _final_norm_kernel(x_ref, g_ref, o_ref):
    x = x_ref[...]
    o_ref[...] = x * lax.rsqrt(jnp.mean(x * x, axis=-1, keepdims=True) + EPS) * g_ref[...]


def _final_norm(x, gain, row_off, rows):
    d = x.shape[1]
    tm = NORM_ROWS
    r0 = row_off // tm
    return pl.pallas_call(
        _final_norm_kernel, out_shape=jax.ShapeDtypeStruct((rows, d), F32), grid=(rows // tm,),
        in_specs=[pl.BlockSpec((tm, d), lambda i: (r0 + i, 0)), pl.BlockSpec((1, d), lambda i: (0, 0))],
        out_specs=pl.BlockSpec((tm, d), lambda i: (i, 0)),
        compiler_params=_params("arbitrary"), name="final_norm",
    )(x, gain)


def _proj_kernel(a_ref, b_ref, cs_ref, o_ref):
    o_ref[...] = (_dot(a_ref[...], b_ref[...]) * cs_ref[...]).astype(o_ref.dtype)


def _proj(h, w, layer, col_scale, tm, tn, name):
    t, k = h.shape
    n = w.shape[2]
    return pl.pallas_call(
        _proj_kernel, out_shape=jax.ShapeDtypeStruct((t, n), BF16), grid=(t // tm, n // tn),
        in_specs=[pl.BlockSpec((tm, k), lambda i, j: (i, 0)),
                  pl.BlockSpec((None, k, tn), lambda i, j: (layer, 0, j)),
                  pl.BlockSpec((1, tn), lambda i, j: (0, j))],
        out_specs=pl.BlockSpec((tm, tn), lambda i, j: (i, j)),
        compiler_params=_params("arbitrary", "arbitrary"), name=name,
    )(h, w, col_scale)


def _merge_kernel(oa_ref, ob_ref, wa_ref, wb_ref, ga_ref, gb_ref, o_ref):
    ya = _dot(oa_ref[...], wa_ref[...])
    yb = _dot(ob_ref[...], wb_ref[...])
    o_ref[...] = (_sigmoid(ga_ref[...].astype(F32)) * ya + _sigmoid(gb_ref[...].astype(F32)) * yb).astype(BF16)


def _merge(oa, ob, w_a, w_b, layer, gates, gate_a_col, gate_b_col, tm, tn):
    t, ka = oa.shape
    kb = ob.shape[1]
    n = w_a.shape[2]
    ja, jb = gate_a_col // tn, gate_b_col // tn
    return pl.pallas_call(
        _merge_kernel, out_shape=jax.ShapeDtypeStruct((t, n), BF16), grid=(t // tm, n // tn),
        in_specs=[
            pl.BlockSpec((tm, ka), lambda i, j: (i, 0)),
            pl.BlockSpec((tm, kb), lambda i, j: (i, 0)),
            pl.BlockSpec((None, ka, tn), lambda i, j: (layer, 0, j)),
            pl.BlockSpec((None, kb, tn), lambda i, j: (layer, 0, j)),
            pl.BlockSpec((tm, tn), lambda i, j: (i, ja + j)),
            pl.BlockSpec((tm, tn), lambda i, j: (i, jb + j)),
        ],
        out_specs=pl.BlockSpec((tm, tn), lambda i, j: (i, j)),
        compiler_params=_params("arbitrary", "arbitrary"), name="merge",
    )(oa, ob, w_a, w_b, gates, gates)


def _resid_kernel(a_ref, b_ref, x_ref, gt_ref, o_ref):
    o_ref[...] = x_ref[...] + gt_ref[...] * _dot(a_ref[...], b_ref[...])


def _resid_matmul(a, w, layer, x, mod, gate_idx, seg_of_row_tile, tm, tn, name, weights_outer=False):
    t, k = a.shape
    n = w.shape[2]
    seg = functools.partial(seg_of_row_tile, tm)
    if weights_outer:
        grid, ij = (n // tn, t // tm), (lambda g0, g1: (g1, g0))
    else:
        grid, ij = (t // tm, n // tn), (lambda g0, g1: (g0, g1))

    def spec(shape, f):
        return pl.BlockSpec(shape, lambda g0, g1: f(*ij(g0, g1)))

    return pl.pallas_call(
        _resid_kernel, out_shape=jax.ShapeDtypeStruct((t, n), F32), grid=grid,
        in_specs=[
            spec((tm, k), lambda i, j: (i, 0)),
            spec((None, k, tn), lambda i, j: (layer, 0, j)),
            spec((tm, tn), lambda i, j: (i, j)),
            spec((None, None, 1, tn), lambda i, j: (seg(i), gate_idx, 0, j)),
        ],
        out_specs=spec((tm, tn), lambda i, j: (i, j)),
        compiler_params=_params("arbitrary", "arbitrary"), name=name,
    )(a, w, x, mod)


def _ffn_in_kernel(a_ref, wg_ref, wu_ref, o_ref):
    a = a_ref[...]
    g = _dot(a, wg_ref[...])
    u = _dot(a, wu_ref[...])
    o_ref[...] = (g * _sigmoid(g) * u).astype(BF16)


def _ffn_in(h, w, layer, d_ff, tm, tn):
    t, k = h.shape
    nj = d_ff // tn
    return pl.pallas_call(
        _ffn_in_kernel, out_shape=jax.ShapeDtypeStruct((t, d_ff), BF16), grid=(t // tm, nj),
        in_specs=[
            pl.BlockSpec((tm, k), lambda i, j: (i, 0)),
            pl.BlockSpec((None, k, tn), lambda i, j: (layer, 0, j)),
            pl.BlockSpec((None, k, tn), lambda i, j: (layer, 0, nj + j)),
        ],
        out_specs=pl.BlockSpec((tm, tn), lambda i, j: (i, j)),
        compiler_params=_params("arbitrary", "arbitrary"), name="ffn_in",
    )(h, w, w)


def _t5_bucket(rel):
    half = REL_BUCKETS // 2
    ret = jnp.where(rel > 0, half, 0)
    n = jnp.abs(rel)
    max_exact = half // 2
    nf = jnp.maximum(n, 1).astype(jnp.float32)
    large = max_exact + (jnp.log(nf / max_exact) / math.log(REL_MAX_DIST / max_exact)
                         * (half - max_exact)).astype(jnp.int32)
    large = jnp.minimum(large, half - 1)
    return ret + jnp.where(n < max_exact, n, large)


def _rel_bias_windows(rel_bias, tile):
    o = jnp.arange(N_BIAS_TILES, dtype=jnp.int32)[:, None]
    e = jnp.arange(2 * tile, dtype=jnp.int32)[None, :]
    rel = (o - N_BIAS_TILES // 2) * tile - (tile - 1) + e
    rows = rel_bias.astype(F32)[_t5_bucket(rel)] * LOG2E
    return jnp.transpose(rows, (2, 0, 1)).reshape(rel_bias.shape[1], N_BIAS_TILES, 1, 2 * tile)


def _bias_tile_kernel(win_ref, o_ref, *, tile):
    window = jnp.broadcast_to(win_ref[...], (tile, 2 * tile))
    o_ref[...] = pltpu.roll(window, tile + 1, 1, stride=1, stride_axis=0)[:, :tile]


def _bias_tiles(rel_bias, tile):
    assert tile >= REL_MAX_DIST
    heads = rel_bias.shape[1]
    win = _rel_bias_windows(rel_bias, tile)
    return pl.pallas_call(
        functools.partial(_bias_tile_kernel, tile=tile),
        out_shape=jax.ShapeDtypeStruct((heads, N_BIAS_TILES, tile, tile), F32),
        grid=(heads, N_BIAS_TILES),
        in_specs=[pl.BlockSpec((None, None, 1, 2 * tile), lambda h, o: (h, o, 0, 0))],
        out_specs=pl.BlockSpec((None, None, tile, tile), lambda h, o: (h, o, 0, 0)),
        compiler_params=_params("arbitrary", "arbitrary"), name="bias_tiles",
    )(win)


def _attn_kernel(lq1_ref, lk1_ref, lq2_ref, lk2_ref, bias_ref, q_ref, k_ref, v_ref, dn_ref,
                 _prev_out_ref, o_ref, m_sc, l_sc, acc_sc, *, lam_init, tile, n_key_tiles, unroll):
    qi = pl.program_id(2)
    dh = DIFF_HEAD_DIM
    hw = 2 * dh
    half = N_BIAS_TILES // 2
    m_sc[...] = jnp.full(m_sc.shape, -jnp.inf, F32)
    l_sc[...] = jnp.zeros(l_sc.shape, F32)
    acc_sc[...] = jnp.zeros(acc_sc.shape, F32)

    def body(kk, carry):
        key_tiles = [kk * unroll + u for u in range(unroll)]
        offs = [pl.multiple_of(kt * tile, tile) for kt in key_tiles]
        bsel = [jnp.clip(kt - qi, -half, half) + half for kt in key_tiles]
        vs = [v_ref[pl.ds(off, tile), :] for off in offs]
        for m in range(2):
            ks = [k_ref[pl.ds(off, tile), m * dh:(m + 1) * dh] for off in offs]
            for r in range(0, tile, ATTN_SUB):
                rows = slice(r, r + ATTN_SUB)
                q = q_ref[rows, m * dh:(m + 1) * dh]
                ss = [_dot_nt(q, ks[u]) + bias_ref[bsel[u], rows, :] for u in range(unroll)]
                m_old = m_sc[m, rows, :]
                m_new = m_old
                for s in ss:
                    m_new = jnp.maximum(m_new, jnp.max(s, axis=-1, keepdims=True))
                alpha = jnp.exp2(m_old - m_new)
                m_rep = _lane_repeat(m_new, tile // LANES)
                l_new = alpha * l_sc[m, rows, :]
                acc = _lane_repeat(alpha, hw // LANES) * acc_sc[m, rows, :]
                for u in range(unroll):
                    p = jnp.exp2(ss[u] - m_rep)
                    l_new = l_new + jnp.sum(p, axis=-1, keepdims=True)
                    acc = acc + _dot(p.astype(BF16), vs[u])
                l_sc[m, rows, :] = l_new
                acc_sc[m, rows, :] = acc
                m_sc[m, rows, :] = m_new
        return carry

    lax.fori_loop(0, n_key_tiles // unroll, body, 0)

    lam = (jnp.exp(jnp.sum(lq1_ref[...] * lk1_ref[...], axis=-1, keepdims=True))
           - jnp.exp(jnp.sum(lq2_ref[...] * lk2_ref[...], axis=-1, keepdims=True)) + lam_init)
    o = (acc_sc[0] / _lane_repeat(l_sc[0], hw // LANES)
         - lam * (acc_sc[1] / _lane_repeat(l_sc[1], hw // LANES)))
    o = o * lax.rsqrt(jnp.mean(o * o, axis=-1, keepdims=True) + EPS) * dn_ref[...] * (1.0 - lam_init)
    o_ref[...] = o.astype(BF16)


def _diff_attention(proj, out, row_off, batch, s, bias_tiles, lam_vecs, diff_norm, lam_init, col_q, col_k, col_v):
    tile = bias_tiles.shape[-1]
    nq = s // tile
    unroll = math.gcd(nq, ATTN_UNROLL)
    hw = 2 * DIFF_HEAD_DIM
    r0 = row_off // tile
    seq0 = row_off // s
    cq, ck, cv = col_q // hw, col_k // hw, col_v // hw
    vec_spec = pl.BlockSpec((1, DIFF_HEAD_DIM), lambda b, h, qi: (0, 0))
    kernel = functools.partial(_attn_kernel, lam_init=lam_init, tile=tile, n_key_tiles=nq, unroll=unroll)
    return pl.pallas_call(
        kernel,
        out_shape=jax.ShapeDtypeStruct(out.shape, out.dtype),
        grid=(batch, N_DIFF_HEADS, nq),
        in_specs=[
            vec_spec, vec_spec, vec_spec, vec_spec,
            pl.BlockSpec((None, N_BIAS_TILES, tile, tile), lambda b, h, qi: (h, 0, 0, 0)),
            pl.BlockSpec((tile, hw), lambda b, h, qi: (r0 + b * nq + qi, cq + h)),
            pl.BlockSpec((s, hw), lambda b, h, qi: (seq0 + b, ck + h)),
            pl.BlockSpec((s, hw), lambda b, h, qi: (seq0 + b, cv + h)),
            pl.BlockSpec((1, hw), lambda b, h, qi: (0, 0)),
            pl.BlockSpec(memory_space=pl.ANY),
        ],
        out_specs=pl.BlockSpec((tile, hw), lambda b, h, qi: (r0 + b * nq + qi, h)),
        scratch_shapes=[
            pltpu.VMEM((2, tile, LANES), F32),
            pltpu.VMEM((2, tile, LANES), F32),
            pltpu.VMEM((2, tile, hw), F32),
        ],
        input_output_aliases={9: 0},
        compiler_params=_params("arbitrary", "arbitrary", "arbitrary"),
        name="diff_attention",
    )(*lam_vecs, bias_tiles, proj, proj, proj, diff_norm, out)


def _gla_kernel(*refs, reverse, block, final, heads):
    if final:
        q_ref, k_ref, v_ref, lr_ref, wg_ref, bg_ref, r_ref, of_ref, gn_ref, _prev_out_ref, o_ref, state_sc = refs
    else:
        q_ref, k_ref, v_ref, lr_ref, wg_ref, bg_ref, _prev_out_ref, o_ref, state_sc = refs
    c = GLA_CHUNK
    shift = c.bit_length() - 1
    assert (1 << shift) == c and block % c == 0

    @pl.when(pl.program_id(2) == 0)
    def _():
        state_sc[...] = jnp.zeros(state_sc.shape, F32)

    x = _dot(lr_ref[...].astype(BF16), wg_ref[...]) + bg_ref[...]
    g = (jnp.minimum(x, 0.0) - jnp.log(1.0 + jnp.exp(-jnp.abs(x)))) / GLA_TAU

    row = lax.broadcasted_iota(jnp.int32, (block, block), 0)
    col = lax.broadcasted_iota(jnp.int32, (block, block), 1)
    same_chunk = lax.shift_right_logical(row, shift) == lax.shift_right_logical(col, shift)
    tri = jnp.where(same_chunk & ((col >= row) if reverse else (col <= row)), 1.0, 0.0).astype(BF16)
    g_hi = g.astype(BF16)
    rem = g - g_hi.astype(F32)
    g_mid = rem.astype(BF16)
    g_lo = (rem - g_mid.astype(F32)).astype(BF16)
    b = _dot(tri, g_hi) + _dot(tri, g_mid) + _dot(tri, g_lo)

    kf = k_ref[...].astype(F32)
    q_t_all = (q_ref[...].astype(F32) * jnp.exp(b)).astype(BF16)
    k_t_all = (kf * jnp.exp(-b)).astype(BF16)

    ci = lax.broadcasted_iota(jnp.int32, (c, c), 0)
    cj = lax.broadcasted_iota(jnp.int32, (c, c), 1)
    keep = (cj > ci) if reverse else (cj <= ci)

    n_chunks = block // c
    for ch in (range(n_chunks - 1, -1, -1) if reverse else range(n_chunks)):
        lo, hi = ch * c, (ch + 1) * c
        b_c = b[lo:hi]
        b_last = b_c[0:1] if reverse else b_c[c - 1:c]
        k_s_all = (kf[lo:hi] * jnp.exp(b_last - b_c)).astype(BF16)
        decay = jnp.exp(b_last)
        for hd in range(heads):
            kc = slice(hd * GLA_DK, (hd + 1) * GLA_DK)
            vc = slice(hd * GLA_DV, (hd + 1) * GLA_DV)
            q_t = q_t_all[lo:hi, kc]
            v_c = v_ref[lo:hi, vc]
            attn = jnp.where(keep, _dot_nt(q_t, k_t_all[lo:hi, kc]), 0.0).astype(BF16)
            state = state_sc[hd]
            o = _dot(attn, v_c) + _dot_nt(q_t, state.astype(BF16))
            state_sc[hd] = state * decay[:, kc] + _dot_tn(v_c, k_s_all[:, kc])
            if final:
                o = o + of_ref[lo:hi, vc]
                o = o * lax.rsqrt(jnp.mean(o * o, axis=-1, keepdims=True) + EPS) * gn_ref[...]
                r = r_ref[lo:hi, vc].astype(F32)
                o_ref[lo:hi, vc] = (o * (r * _sigmoid(r))).astype(BF16)
            else:
                o_ref[lo:hi, vc] = o


def _gla_pass(proj, lr, w_gate, b_gate, out, row_off, batch, s, cols, *, reverse, o_fwd=None, gla_norm=None):
    final = o_fwd is not None
    block = min(GLA_BLOCK, s)
    nblk = s // block
    r0 = row_off // block
    col_q, col_k, col_v, col_r = cols
    heads = math.gcd(GLA_HEADS, GLA_HEADS_PER_STEP)
    wk, wv = heads * GLA_DK, heads * GLA_DV
    assert col_q % wk == 0 and col_k % wk == 0 and col_v % wv == 0 and col_r % wv == 0

    def rows(b, t):
        return r0 + b * nblk + ((nblk - 1 - t) if reverse else t)

    in_specs = [
        pl.BlockSpec((block, wk), lambda b, h, t: (rows(b, t), col_q // wk + h)),
        pl.BlockSpec((block, wk), lambda b, h, t: (rows(b, t), col_k // wk + h)),
        pl.BlockSpec((block, wv), lambda b, h, t: (rows(b, t), col_v // wv + h)),
        pl.BlockSpec((block, LANES), lambda b, h, t: (rows(b, t), 0)),
        pl.BlockSpec((LANES, wk), lambda b, h, t: (0, h)),
        pl.BlockSpec((1, wk), lambda b, h, t: (0, h)),
    ]
    args = [proj, proj, proj, lr, w_gate, b_gate]
    if final:
        in_specs += [
            pl.BlockSpec((block, wv), lambda b, h, t: (rows(b, t), col_r // wv + h)),
            pl.BlockSpec((block, wv), lambda b, h, t: (rows(b, t), h)),
            pl.BlockSpec((1, GLA_DV), lambda b, h, t: (0, 0)),
        ]
        args += [proj, o_fwd, gla_norm]
    in_specs.append(pl.BlockSpec(memory_space=pl.ANY))
    args.append(out)
    return pl.pallas_call(
        functools.partial(_gla_kernel, reverse=reverse, block=block, final=final, heads=heads),
        out_shape=jax.ShapeDtypeStruct(out.shape, out.dtype),
        grid=(batch, GLA_HEADS // heads, nblk),
        in_specs=in_specs,
        out_specs=pl.BlockSpec((block, wv), lambda b, h, t: (rows(b, t), h)),
        scratch_shapes=[pltpu.VMEM((heads, GLA_DV, GLA_DK), F32)],
        input_output_aliases={len(args) - 1: 0},
        compiler_params=_params("arbitrary", "arbitrary", "arbitrary"),
        name="gla_bwd" if reverse else "gla_fwd",
    )(*args)


def _tile(n, pref):
    if n <= pref:
        return n
    t = (pref // LANES) * LANES
    while n % t:
        t -= LANES
    return t


def kernel(x_prompt, x_sample, c_prompt, c_sample, rel_bias, w_ada, b_ada, norm_mix, norm_ffn, w_in, lam_q1, lam_k1, lam_q2, lam_k2, diff_norm, w_gate2, b_gate2, gla_norm, w_br_a, w_br_b, w_out, w_ffn_in, w_ffn_out, norm_final):
    depth = w_in.shape[0]
    bp, sp, d = x_prompt.shape
    bs, ss, _ = x_sample.shape
    tp, ts = bp * sp, bs * ss
    d_ff = w_ffn_out.shape[1]
    diff_w = N_DIFF_HEADS * 2 * DIFF_HEAD_DIM
    gla_k, gla_w = GLA_HEADS * GLA_DK, GLA_HEADS * GLA_DV
    n_lr = 2 * GLA_LOWRANK
    assert bp + bs <= MOD_ROWS and n_lr <= LANES

    col_qa, col_ka, col_va = 0, diff_w, 2 * diff_w
    col_qg = 3 * diff_w
    col_kg = col_qg + gla_k
    col_vg = col_kg + gla_k
    col_rg = col_vg + gla_w
    col_lr = col_rg + gla_w
    n_mix = col_lr
    n_gates = 2 * d

    tm = _tile(math.gcd(sp, ss), ROW_TILE)
    tm_ffn = _tile(math.gcd(sp, ss), FFN_IN_ROW_TILE)

    w_mix = w_in[:, :, :n_mix].astype(BF16)
    w_gates = w_in[:, :, col_lr + n_lr:].astype(BF16)
    w_lr_all = jnp.pad(w_in[:, :, col_lr:col_lr + n_lr], ((0, 0), (0, 0), (0, LANES - n_lr))).astype(BF16)
    w_bra, w_brb, w_o = w_br_a.astype(BF16), w_br_b.astype(BF16), w_out.astype(BF16)
    w_fi, w_fo = w_ffn_in.astype(BF16), w_ffn_out.astype(BF16)

    def seg_of_row_tile(rows_per_tile, i):
        r = i * rows_per_tile
        return jnp.where(r < tp, r // sp, bp + (r - tp) // ss)

    x = jnp.concatenate([x_prompt.reshape(tp, d), x_sample.reshape(ts, d)], axis=0)
    c_rows = jnp.zeros((MOD_ROWS, d), F32).at[:bp].set(c_prompt).at[bp:bp + bs].set(c_sample)
    mod = _modulation(c_rows, w_ada, b_ada).reshape(depth, MOD_ROWS, N_MOD, 1, d)

    attn_tile = _tile(math.gcd(sp, ss), ATTN_TILE)
    assert tp % ss == 0 and attn_tile % ATTN_SUB == 0
    bias_tiles = _bias_tiles(rel_bias, attn_tile)
    gla_cols = (col_qg, col_kg, col_vg, col_rg)
    mix_scale = jnp.ones((1, n_mix), F32)
    mix_scale = mix_scale.at[:, col_qa:col_ka].set(DIFF_HEAD_DIM ** -0.5 * LOG2E)
    mix_scale = mix_scale.at[:, col_qg:col_kg].set(GLA_DK ** -0.5)
    gate_scale = jnp.ones((1, n_gates), F32)

    for l in range(depth):
        lam_init = 0.8 - 0.6 * math.exp(-0.3 * l)
        mod_l = mod[l]
        w_lr = w_lr_all[l]
        wg_f = jnp.zeros((LANES, gla_k), F32).at[:GLA_LOWRANK].set(w_gate2[l, 0]).astype(BF16)
        wg_b = jnp.zeros((LANES, gla_k), F32).at[GLA_LOWRANK:n_lr].set(w_gate2[l, 1]).astype(BF16)
        bg_f = b_gate2[l, 0].reshape(1, gla_k)
        bg_b = b_gate2[l, 1].reshape(1, gla_k)
        lam_vecs = [v[l].reshape(1, DIFF_HEAD_DIM) for v in (lam_q1, lam_k1, lam_q2, lam_k2)]

        h, lr = _norm_mod(x, norm_mix[l].reshape(1, d), mod_l, 0, 1, seg_of_row_tile, w_lr=w_lr)
        proj = _proj(h, w_mix, l, mix_scale, tm, _tile(n_mix, 1024), "in_proj_mix")
        gates = _proj(h, w_gates, l, gate_scale, tm, _tile(n_gates, 1024), "in_proj_gates")

        oa = jnp.zeros((tp + ts, diff_w), BF16)
        dn = diff_norm[l].reshape(1, 2 * DIFF_HEAD_DIM)
        oa = _diff_attention(proj, oa, 0, bp, sp, bias_tiles, lam_vecs, dn, lam_init, col_qa, col_ka, col_va)
        oa = _diff_attention(proj, oa, tp, bs, ss, bias_tiles, lam_vecs, dn, lam_init, col_qa, col_ka, col_va)

        o_f = jnp.zeros((tp + ts, gla_w), F32)
        o_f = _gla_pass(proj, lr, wg_f, bg_f, o_f, 0, bp, sp, gla_cols, reverse=False)
        o_f = _gla_pass(proj, lr, wg_f, bg_f, o_f, tp, bs, ss, gla_cols, reverse=False)
        gn = gla_norm[l].reshape(1, GLA_DV)
        ob = jnp.zeros((tp + ts, gla_w), BF16)
        ob = _gla_pass(proj, lr, wg_b, bg_b, ob, 0, bp, sp, gla_cols, reverse=True, o_fwd=o_f, gla_norm=gn)
        ob = _gla_pass(proj, lr, wg_b, bg_b, ob, tp, bs, ss, gla_cols, reverse=True, o_fwd=o_f, gla_norm=gn)

        merged = _merge(oa, ob, w_bra, w_brb, l, gates, 0, d, tm, _tile(d, 512))
        x = _resid_matmul(merged, w_o, l, x, mod_l, 2, seg_of_row_tile, tm, _tile(d, 512), "out_proj")

        h = _norm_mod(x, norm_ffn[l].reshape(1, d), mod_l, 3, 4, seg_of_row_tile)
        act = _ffn_in(h, w_fi, l, d_ff, tm_ffn, _tile(d_ff, 256))
        x = _resid_matmul(act, w_fo, l, x, mod_l, 5, seg_of_row_tile, _tile(tm, 512), _tile(d, 512), "ffn_out",
                          weights_outer=True)

    gain = norm_final.reshape(1, d)
    return (_final_norm(x, gain, 0, tp).reshape(bp, sp, d), _final_norm(x, gain, tp, ts).reshape(bs, ss, d))
```

```python
import functools
import math

import jax
import jax.numpy as jnp
from jax import lax
from jax.experimental import pallas as pl
from jax.experimental.pallas import tpu as pltpu

N_DIFF_HEADS = 8
DIFF_HEAD_DIM = 128
REL_BUCKETS = 32
REL_MAX_DIST = 128
GLA_HEADS = 4
GLA_DK = 256
GLA_DV = 512
GLA_LOWRANK = 16
GLA_TAU = 16.0
GLA_CHUNK = 64
N_MOD = 6
EPS = 1e-6

LANES = 128
VMEM_LIMIT_BYTES = 56 * 1024 * 1024

ROW_TILE = 1024
FFN_IN_ROW_TILE = 2048
NORM_ROWS = 256
ATTN_TILE = 512
ATTN_SUB = 128
ATTN_UNROLL = 4
N_BIAS_TILES = 5
GLA_BLOCK = 256
GLA_HEADS_PER_STEP = 4
LOG2E = 1.4426950408889634
MOD_ROWS = 16

F32 = jnp.float32
BF16 = jnp.bfloat16


def _params(*sem):
    return pltpu.CompilerParams(dimension_semantics=sem, vmem_limit_bytes=VMEM_LIMIT_BYTES)


def _dot(a, b):
    return jnp.dot(a, b, preferred_element_type=F32)


def _dot_nt(a, b):
    return lax.dot_general(a, b, (((1,), (1,)), ((), ())), preferred_element_type=F32)


def _dot_tn(a, b):
    return lax.dot_general(a, b, (((0,), (0,)), ((), ())), preferred_element_type=F32)


def _sigmoid(x):
    return 0.5 * jnp.tanh(0.5 * x) + 0.5


def _lane_repeat(x, n):
    return jnp.concatenate([x] * n, axis=1) if n > 1 else x


def _write_into(kernel, in_specs, args, out):
    if isinstance(out, jax.ShapeDtypeStruct):
        return kernel, in_specs, args, {}
    index = len(args)

    def body(*refs):
        return kernel(*refs[:index], *refs[index + 1:])

    return body, in_specs + [pl.BlockSpec(memory_space=pl.ANY)], args + [out], {index: 0}


def _mod_kernel(c_ref, w_ref, b_ref, o_ref):
    @pl.when(pl.program_id(1) == 0)
    def _():
        o_ref[...] = jnp.broadcast_to(b_ref[...], o_ref.shape)

    c = c_ref[...]
    a = (c * _sigmoid(c)).astype(BF16)
    o_ref[...] += _dot(a, w_ref[...].astype(BF16))


def _modulation(c_rows, w_ada, b_ada):
    depth, d, n = w_ada.shape
    tk = LANES
    return pl.pallas_call(
        _mod_kernel,
        out_shape=jax.ShapeDtypeStruct((depth, MOD_ROWS, n), F32),
        grid=(depth, d // tk),
        in_specs=[
            pl.BlockSpec((MOD_ROWS, tk), lambda l, k: (0, k)),
            pl.BlockSpec((None, tk, n), lambda l, k: (l, k, 0)),
            pl.BlockSpec((None, 1, n), lambda l, k: (l, 0, 0)),
        ],
        out_specs=pl.BlockSpec((None, MOD_ROWS, n), lambda l, k: (l, 0, 0)),
        compiler_params=_params("arbitrary", "arbitrary"),
        name="modulation",
    )(c_rows, w_ada, b_ada.reshape(depth, 1, n))


def _norm_mod_kernel(x_ref, g_ref, sc_ref, sh_ref, h_ref):
    x = x_ref[...]
    y = x * lax.rsqrt(jnp.mean(x * x, axis=-1, keepdims=True) + EPS) * g_ref[...]
    h_ref[...] = (y * (1.0 + sc_ref[...]) + sh_ref[...]).astype(BF16)


def _norm_mod_lr_kernel(x_ref, g_ref, sc_ref, sh_ref, wlr_ref, h_ref, lr_ref):
    x = x_ref[...]
    y = x * lax.rsqrt(jnp.mean(x * x, axis=-1, keepdims=True) + EPS) * g_ref[...]
    h = (y * (1.0 + sc_ref[...]) + sh_ref[...]).astype(BF16)
    h_ref[...] = h
    lr_ref[...] = _dot(h, wlr_ref[...])


def _norm_mod(x, gain, mod, shift_idx, scale_idx, seg_of_row_tile, w_lr=None):
    t, d = x.shape
    tm = NORM_ROWS
    seg = functools.partial(seg_of_row_tile, tm)
    in_specs = [
        pl.BlockSpec((tm, d), lambda i: (i, 0)),
        pl.BlockSpec((1, d), lambda i: (0, 0)),
        pl.BlockSpec((None, None, 1, d), lambda i: (seg(i), scale_idx, 0, 0)),
        pl.BlockSpec((None, None, 1, d), lambda i: (seg(i), shift_idx, 0, 0)),
    ]
    h_spec = pl.BlockSpec((tm, d), lambda i: (i, 0))
    h_shape = jax.ShapeDtypeStruct((t, d), BF16)
    if w_lr is None:
        return pl.pallas_call(
            _norm_mod_kernel, out_shape=h_shape, grid=(t // tm,), in_specs=in_specs, out_specs=h_spec,
            compiler_params=_params("arbitrary"), name="norm_mod",
        )(x, gain, mod, mod)
    return pl.pallas_call(
        _norm_mod_lr_kernel,
        out_shape=(h_shape, jax.ShapeDtypeStruct((t, LANES), F32)),
        grid=(t // tm,),
        in_specs=in_specs + [pl.BlockSpec((d, LANES), lambda i: (0, 0))],
        out_specs=(h_spec, pl.BlockSpec((tm, LANES), lambda i: (i, 0))),
        compiler_params=_params("arbitrary"), name="norm_mod_lr",
    )(x, gain, mod, mod, w_lr)


def _final_norm_kernel(x_ref, g_ref, o_ref):
    x = x_ref[...]
    o_ref[...] = x * lax.rsqrt(jnp.mean(x * x, axis=-1, keepdims=True) + EPS) * g_ref[...]


def _final_norm(x, gain, row_off, rows):
    d = x.shape[1]
    tm = NORM_ROWS
    r0 = row_off // tm
    return pl.pallas_call(
        _final_norm_kernel, out_shape=jax.ShapeDtypeStruct((rows, d), F32), grid=(rows // tm,),
        in_specs=[pl.BlockSpec((tm, d), lambda i: (r0 + i, 0)), pl.BlockSpec((1, d), lambda i: (0, 0))],
        out_specs=pl.BlockSpec((tm, d), lambda i: (i, 0)),
        compiler_params=_params("arbitrary"), name="final_norm",
    )(x, gain)


def _proj_kernel(a_ref, b_ref, cs_ref, o_ref):
    o_ref[...] = (_dot(a_ref[...], b_ref[...]) * cs_ref[...]).astype(o_ref.dtype)


def _proj(h, w, layer, col_scale, tm, tn, name):
    t, k = h.shape
    n = col_scale.shape[1]
    assert n % tn == 0 and n <= w.shape[2]
    return pl.pallas_call(
        _proj_kernel, out_shape=jax.ShapeDtypeStruct((t, n), BF16), grid=(t // tm, n // tn),
        in_specs=[pl.BlockSpec((tm, k), lambda i, j: (i, 0)),
                  pl.BlockSpec((None, k, tn), lambda i, j: (layer, 0, j)),
                  pl.BlockSpec((1, tn), lambda i, j: (0, j))],
        out_specs=pl.BlockSpec((tm, tn), lambda i, j: (i, j)),
        compiler_params=_params("arbitrary", "arbitrary"), name=name,
    )(h, w, col_scale)


def _merge_kernel(oa_ref, ob_ref, wa_ref, wb_ref, ga_ref, gb_ref, o_ref):
    ya = _dot(oa_ref[...], wa_ref[...])
    yb = _dot(ob_ref[...], wb_ref[...])
    o_ref[...] = (_sigmoid(ga_ref[...].astype(F32)) * ya + _sigmoid(gb_ref[...].astype(F32)) * yb).astype(BF16)


def _merge(oa, ob, w_a, w_b, layer, gates, gate_a_col, gate_b_col, tm, tn):
    t, ka = oa.shape
    kb = ob.shape[1]
    n = w_a.shape[2]
    ja, jb = gate_a_col // tn, gate_b_col // tn
    return pl.pallas_call(
        _merge_kernel, out_shape=jax.ShapeDtypeStruct((t, n), BF16), grid=(t // tm, n // tn),
        in_specs=[
            pl.BlockSpec((tm, ka), lambda i, j: (i, 0)),
            pl.BlockSpec((tm, kb), lambda i, j: (i, 0)),
            pl.BlockSpec((None, ka, tn), lambda i, j: (layer, 0, j)),
            pl.BlockSpec((None, kb, tn), lambda i, j: (layer, 0, j)),
            pl.BlockSpec((tm, tn), lambda i, j: (i, ja + j)),
            pl.BlockSpec((tm, tn), lambda i, j: (i, jb + j)),
        ],
        out_specs=pl.BlockSpec((tm, tn), lambda i, j: (i, j)),
        compiler_params=_params("arbitrary", "arbitrary"), name="merge",
    )(oa, ob, w_a, w_b, gates, gates)


def _resid_kernel(a_ref, b_ref, x_ref, gt_ref, o_ref):
    o_ref[...] = x_ref[...] + gt_ref[...] * _dot(a_ref[...], b_ref[...])


def _resid_matmul(a, w, layer, x, mod, gate_idx, seg_of_row_tile, tm, tn, name, weights_outer=False):
    t, k = a.shape
    n = w.shape[2]
    seg = functools.partial(seg_of_row_tile, tm)
    if weights_outer:
        grid, ij = (n // tn, t // tm), (lambda g0, g1: (g1, g0))
    else:
        grid, ij = (t // tm, n // tn), (lambda g0, g1: (g0, g1))

    def spec(shape, f):
        return pl.BlockSpec(shape, lambda g0, g1: f(*ij(g0, g1)))

    return pl.pallas_call(
        _resid_kernel, out_shape=jax.ShapeDtypeStruct((t, n), F32), grid=grid,
        in_specs=[
            spec((tm, k), lambda i, j: (i, 0)),
            spec((None, k, tn), lambda i, j: (layer, 0, j)),
            spec((tm, tn), lambda i, j: (i, j)),
            spec((None, None, 1, tn), lambda i, j: (seg(i), gate_idx, 0, j)),
        ],
        out_specs=spec((tm, tn), lambda i, j: (i, j)),
        compiler_params=_params("arbitrary", "arbitrary"), name=name,
    )(a, w, x, mod)


def _ffn_in_kernel(a_ref, wg_ref, wu_ref, o_ref):
    a = a_ref[...]
    g = _dot(a, wg_ref[...])
    u = _dot(a, wu_ref[...])
    o_ref[...] = (g * _sigmoid(g) * u).astype(BF16)


def _ffn_in(h, w, layer, d_ff, tm, tn):
    t, k = h.shape
    nj = d_ff // tn
    return pl.pallas_call(
        _ffn_in_kernel, out_shape=jax.ShapeDtypeStruct((t, d_ff), BF16), grid=(t // tm, nj),
        in_specs=[
            pl.BlockSpec((tm, k), lambda i, j: (i, 0)),
            pl.BlockSpec((None, k, tn), lambda i, j: (layer, 0, j)),
            pl.BlockSpec((None, k, tn), lambda i, j: (layer, 0, nj + j)),
        ],
        out_specs=pl.BlockSpec((tm, tn), lambda i, j: (i, j)),
        compiler_params=_params("arbitrary", "arbitrary"), name="ffn_in",
    )(h, w, w)


def _t5_bucket(rel):
    half = REL_BUCKETS // 2
    ret = jnp.where(rel > 0, half, 0)
    n = jnp.abs(rel)
    max_exact = half // 2
    nf = jnp.maximum(n, 1).astype(jnp.float32)
    large = max_exact + (jnp.log(nf / max_exact) / math.log(REL_MAX_DIST / max_exact)
                         * (half - max_exact)).astype(jnp.int32)
    large = jnp.minimum(large, half - 1)
    return ret + jnp.where(n < max_exact, n, large)


def _rel_bias_windows(rel_bias, tile):
    o = jnp.arange(N_BIAS_TILES, dtype=jnp.int32)[:, None]
    e = jnp.arange(2 * tile, dtype=jnp.int32)[None, :]
    rel = (o - N_BIAS_TILES // 2) * tile - (tile - 1) + e
    rows = rel_bias.astype(F32)[_t5_bucket(rel)] * LOG2E
    return jnp.transpose(rows, (2, 0, 1)).reshape(rel_bias.shape[1], N_BIAS_TILES, 1, 2 * tile)


def _bias_tile_kernel(win_ref, o_ref, *, tile):
    window = jnp.broadcast_to(win_ref[...], (tile, 2 * tile))
    o_ref[...] = pltpu.roll(window, tile + 1, 1, stride=1, stride_axis=0)[:, :tile]


def _bias_tiles(rel_bias, tile):
    assert tile >= REL_MAX_DIST
    heads = rel_bias.shape[1]
    win = _rel_bias_windows(rel_bias, tile)
    return pl.pallas_call(
        functools.partial(_bias_tile_kernel, tile=tile),
        out_shape=jax.ShapeDtypeStruct((heads, N_BIAS_TILES, tile, tile), F32),
        grid=(heads, N_BIAS_TILES),
        in_specs=[pl.BlockSpec((None, None, 1, 2 * tile), lambda h, o: (h, o, 0, 0))],
        out_specs=pl.BlockSpec((None, None, tile, tile), lambda h, o: (h, o, 0, 0)),
        compiler_params=_params("arbitrary", "arbitrary"), name="bias_tiles",
    )(win)


def _attn_kernel(lq1_ref, lk1_ref, lq2_ref, lk2_ref, bias_ref, q_ref, k_ref, v_ref, dn_ref,
                 o_ref, m_sc, l_sc, acc_sc, *, lam_init, tile, n_key_tiles, unroll):
    qi = pl.program_id(2)
    dh = DIFF_HEAD_DIM
    hw = 2 * dh
    half = N_BIAS_TILES // 2

    def fold(kk, first):
        key_tiles = [kk * unroll + u for u in range(unroll)]
        offs = [kt * tile if isinstance(kt, int) else pl.multiple_of(kt * tile, tile) for kt in key_tiles]
        bsel = [jnp.clip(kt - qi, -half, half) + half for kt in key_tiles]
        vs = [v_ref[pl.ds(off, tile), :] for off in offs]
        for m in range(2):
            ks = [k_ref[pl.ds(off, tile), m * dh:(m + 1) * dh] for off in offs]
            for r in range(0, tile, ATTN_SUB):
                rows = slice(r, r + ATTN_SUB)
                q = q_ref[rows, m * dh:(m + 1) * dh]
                ss = [_dot_nt(q, ks[u]) + bias_ref[bsel[u], rows, :] for u in range(unroll)]
                tile_max = [jnp.max(s, axis=-1, keepdims=True) for s in ss]
                if first:
                    m_new = jnp.broadcast_to(functools.reduce(jnp.maximum, tile_max), (ATTN_SUB, LANES))
                    l_new, acc = None, None
                else:
                    m_old = m_sc[m, rows, :]
                    m_new = functools.reduce(jnp.maximum, tile_max, m_old)
                    alpha = jnp.exp2(m_old - m_new)
                    l_new = alpha * l_sc[m, rows, :]
                    acc = _lane_repeat(alpha, hw // LANES) * acc_sc[m, rows, :]
                m_rep = _lane_repeat(m_new, tile // LANES)
                for u in range(unroll):
                    p = jnp.exp2(ss[u] - m_rep)
                    p_sum = jnp.sum(p, axis=-1, keepdims=True)
                    pv = _dot(p.astype(BF16), vs[u])
                    l_new = p_sum if l_new is None else l_new + p_sum
                    acc = pv if acc is None else acc + pv
                l_sc[m, rows, :] = jnp.broadcast_to(l_new, (ATTN_SUB, LANES))
                acc_sc[m, rows, :] = acc
                m_sc[m, rows, :] = m_new

    fold(0, True)
    n_trips = n_key_tiles // unroll
    if n_trips > 1:
        def body(kk, carry):
            fold(kk, False)
            return carry

        lax.fori_loop(1, n_trips, body, 0)

    lam = (jnp.exp(jnp.sum(lq1_ref[...] * lk1_ref[...], axis=-1, keepdims=True))
           - jnp.exp(jnp.sum(lq2_ref[...] * lk2_ref[...], axis=-1, keepdims=True)) + lam_init)
    o = (acc_sc[0] / _lane_repeat(l_sc[0], hw // LANES)
         - lam * (acc_sc[1] / _lane_repeat(l_sc[1], hw // LANES)))
    o = o * lax.rsqrt(jnp.mean(o * o, axis=-1, keepdims=True) + EPS) * dn_ref[...] * (1.0 - lam_init)
    o_ref[...] = o.astype(BF16)


def _diff_attention(proj, out, row_off, batch, s, bias_tiles, lam_vecs, diff_norm, lam_init, col_q, col_k, col_v):
    tile = bias_tiles.shape[-1]
    nq = s // tile
    unroll = math.gcd(nq, ATTN_UNROLL)
    hw = 2 * DIFF_HEAD_DIM
    r0 = row_off // tile
    seq0 = row_off // s
    cq, ck, cv = col_q // hw, col_k // hw, col_v // hw
    vec_spec = pl.BlockSpec((1, DIFF_HEAD_DIM), lambda b, h, qi: (0, 0))
    kernel = functools.partial(_attn_kernel, lam_init=lam_init, tile=tile, n_key_tiles=nq, unroll=unroll)
    in_specs = [
        vec_spec, vec_spec, vec_spec, vec_spec,
        pl.BlockSpec((None, N_BIAS_TILES, tile, tile), lambda b, h, qi: (h, 0, 0, 0)),
        pl.BlockSpec((tile, hw), lambda b, h, qi: (r0 + b * nq + qi, cq + h)),
        pl.BlockSpec((s, hw), lambda b, h, qi: (seq0 + b, ck + h)),
        pl.BlockSpec((s, hw), lambda b, h, qi: (seq0 + b, cv + h)),
        pl.BlockSpec((1, hw), lambda b, h, qi: (0, 0)),
    ]
    args = [*lam_vecs, bias_tiles, proj, proj, proj, diff_norm]
    kernel, in_specs, args, aliases = _write_into(kernel, in_specs, args, out)
    return pl.pallas_call(
        kernel,
        out_shape=jax.ShapeDtypeStruct(out.shape, out.dtype),
        grid=(batch, N_DIFF_HEADS, nq),
        in_specs=in_specs,
        out_specs=pl.BlockSpec((tile, hw), lambda b, h, qi: (r0 + b * nq + qi, h)),
        scratch_shapes=[
            pltpu.VMEM((2, tile, LANES), F32),
            pltpu.VMEM((2, tile, LANES), F32),
            pltpu.VMEM((2, tile, hw), F32),
        ],
        input_output_aliases=aliases,
        compiler_params=_params("arbitrary", "arbitrary", "arbitrary"),
        name="diff_attention",
    )(*args)


def _gla_kernel(*refs, reverse, block, final, heads):
    if final:
        q_ref, k_ref, v_ref, lr_ref, wg_ref, bg_ref, r_ref, of_ref, gn_ref, o_ref, state_sc = refs
    else:
        q_ref, k_ref, v_ref, lr_ref, wg_ref, bg_ref, o_ref, state_sc = refs
    c = GLA_CHUNK
    shift = c.bit_length() - 1
    assert (1 << shift) == c and block % c == 0

    @pl.when(pl.program_id(2) == 0)
    def _():
        state_sc[...] = jnp.zeros(state_sc.shape, F32)

    x = _dot(lr_ref[...].astype(BF16), wg_ref[...]) + bg_ref[...]
    g = (jnp.minimum(x, 0.0) - jnp.log(1.0 + jnp.exp(-jnp.abs(x)))) / GLA_TAU

    row = lax.broadcasted_iota(jnp.int32, (block, block), 0)
    col = lax.broadcasted_iota(jnp.int32, (block, block), 1)
    same_chunk = lax.shift_right_logical(row, shift) == lax.shift_right_logical(col, shift)
    tri = jnp.where(same_chunk & ((col >= row) if reverse else (col <= row)), 1.0, 0.0).astype(BF16)
    g_hi = g.astype(BF16)
    rem = g - g_hi.astype(F32)
    g_mid = rem.astype(BF16)
    g_lo = (rem - g_mid.astype(F32)).astype(BF16)
    b = _dot(tri, g_hi) + _dot(tri, g_mid) + _dot(tri, g_lo)

    kf = k_ref[...].astype(F32)
    q_t_all = (q_ref[...].astype(F32) * jnp.exp(b)).astype(BF16)
    k_t_all = (kf * jnp.exp(-b)).astype(BF16)

    ci = lax.broadcasted_iota(jnp.int32, (c, c), 0)
    cj = lax.broadcasted_iota(jnp.int32, (c, c), 1)
    keep = (cj > ci) if reverse else (cj <= ci)

    n_chunks = block // c
    for ch in (range(n_chunks - 1, -1, -1) if reverse else range(n_chunks)):
        lo, hi = ch * c, (ch + 1) * c
        b_c = b[lo:hi]
        b_last = b_c[0:1] if reverse else b_c[c - 1:c]
        k_s_all = (kf[lo:hi] * jnp.exp(b_last - b_c)).astype(BF16)
        decay = jnp.exp(b_last)
        for hd in range(heads):
            kc = slice(hd * GLA_DK, (hd + 1) * GLA_DK)
            vc = slice(hd * GLA_DV, (hd + 1) * GLA_DV)
            q_t = q_t_all[lo:hi, kc]
            v_c = v_ref[lo:hi, vc]
            attn = jnp.where(keep, _dot_nt(q_t, k_t_all[lo:hi, kc]), 0.0).astype(BF16)
            state = state_sc[hd]
            o = _dot(attn, v_c) + _dot_nt(q_t, state.astype(BF16))
            state_sc[hd] = state * decay[:, kc] + _dot_tn(v_c, k_s_all[:, kc])
            if final:
                o = o + of_ref[lo:hi, vc]
                o = o * lax.rsqrt(jnp.mean(o * o, axis=-1, keepdims=True) + EPS) * gn_ref[...]
                r = r_ref[lo:hi, vc].astype(F32)
                o_ref[lo:hi, vc] = (o * (r * _sigmoid(r))).astype(BF16)
            else:
                o_ref[lo:hi, vc] = o


def _gla_pass(proj, lr, w_gate, b_gate, out, row_off, batch, s, cols, *, reverse, o_fwd=None, gla_norm=None):
    final = o_fwd is not None
    block = min(GLA_BLOCK, s)
    nblk = s // block
    r0 = row_off // block
    col_q, col_k, col_v, col_r = cols
    heads = math.gcd(GLA_HEADS, GLA_HEADS_PER_STEP)
    wk, wv = heads * GLA_DK, heads * GLA_DV
    assert col_q % wk == 0 and col_k % wk == 0 and col_v % wv == 0 and col_r % wv == 0

    def rows(b, t):
        return r0 + b * nblk + ((nblk - 1 - t) if reverse else t)

    in_specs = [
        pl.BlockSpec((block, wk), lambda b, h, t: (rows(b, t), col_q // wk + h)),
        pl.BlockSpec((block, wk), lambda b, h, t: (rows(b, t), col_k // wk + h)),
        pl.BlockSpec((block, wv), lambda b, h, t: (rows(b, t), col_v // wv + h)),
        pl.BlockSpec((block, LANES), lambda b, h, t: (rows(b, t), 0)),
        pl.BlockSpec((LANES, wk), lambda b, h, t: (0, h)),
        pl.BlockSpec((1, wk), lambda b, h, t: (0, h)),
    ]
    args = [proj, proj, proj, lr, w_gate, b_gate]
    if final:
        in_specs += [
            pl.BlockSpec((block, wv), lambda b, h, t: (rows(b, t), col_r // wv + h)),
            pl.BlockSpec((block, wv), lambda b, h, t: (rows(b, t), h)),
            pl.BlockSpec((1, GLA_DV), lambda b, h, t: (0, 0)),
        ]
        args += [proj, o_fwd, gla_norm]
    kernel = functools.partial(_gla_kernel, reverse=reverse, block=block, final=final, heads=heads)
    kernel, in_specs, args, aliases = _write_into(kernel, in_specs, args, out)
    return pl.pallas_call(
        kernel,
        out_shape=jax.ShapeDtypeStruct(out.shape, out.dtype),
        grid=(batch, GLA_HEADS // heads, nblk),
        in_specs=in_specs,
        out_specs=pl.BlockSpec((block, wv), lambda b, h, t: (rows(b, t), h)),
        scratch_shapes=[pltpu.VMEM((heads, GLA_DV, GLA_DK), F32)],
        input_output_aliases=aliases,
        compiler_params=_params("arbitrary", "arbitrary", "arbitrary"),
        name="gla_bwd" if reverse else "gla_fwd",
    )(*args)


def _tile(n, pref):
    if n <= pref:
        return n
    t = (pref // LANES) * LANES
    while n % t:
        t -= LANES
    return t


def kernel(x_prompt, x_sample, c_prompt, c_sample, rel_bias, w_ada, b_ada, norm_mix, norm_ffn, w_in, lam_q1, lam_k1, lam_q2, lam_k2, diff_norm, w_gate2, b_gate2, gla_norm, w_br_a, w_br_b, w_out, w_ffn_in, w_ffn_out, norm_final):
    depth = w_in.shape[0]
    bp, sp, d = x_prompt.shape
    bs, ss, _ = x_sample.shape
    tp, ts = bp * sp, bs * ss
    d_ff = w_ffn_out.shape[1]
    diff_w = N_DIFF_HEADS * 2 * DIFF_HEAD_DIM
    gla_k, gla_w = GLA_HEADS * GLA_DK, GLA_HEADS * GLA_DV
    n_lr = 2 * GLA_LOWRANK
    assert bp + bs <= MOD_ROWS and n_lr <= LANES

    col_qa, col_ka, col_va = 0, diff_w, 2 * diff_w
    col_qg = 3 * diff_w
    col_kg = col_qg + gla_k
    col_vg = col_kg + gla_k
    col_rg = col_vg + gla_w
    col_lr = col_rg + gla_w
    n_mix = col_lr
    n_gates = 2 * d

    tm = _tile(math.gcd(sp, ss), ROW_TILE)
    tm_ffn = _tile(math.gcd(sp, ss), FFN_IN_ROW_TILE)

    w_in_bf = w_in.astype(BF16)
    w_gates = w_in_bf[:, :, col_lr + n_lr:]
    w_lr_all = jnp.pad(w_in_bf[:, :, col_lr:col_lr + n_lr], ((0, 0), (0, 0), (0, LANES - n_lr)))
    w_bra, w_brb, w_o = w_br_a.astype(BF16), w_br_b.astype(BF16), w_out.astype(BF16)
    w_fi, w_fo = w_ffn_in.astype(BF16), w_ffn_out.astype(BF16)

    def seg_of_row_tile(rows_per_tile, i):
        r = i * rows_per_tile
        return jnp.where(r < tp, r // sp, bp + (r - tp) // ss)

    x = jnp.concatenate([x_prompt.reshape(tp, d), x_sample.reshape(ts, d)], axis=0)
    c_rows = jnp.zeros((MOD_ROWS, d), F32).at[:bp].set(c_prompt).at[bp:bp + bs].set(c_sample)
    mod = _modulation(c_rows, w_ada, b_ada).reshape(depth, MOD_ROWS, N_MOD, 1, d)

    attn_tile = _tile(math.gcd(sp, ss), ATTN_TILE)
    assert tp % ss == 0 and attn_tile % ATTN_SUB == 0
    bias_tiles = _bias_tiles(rel_bias, attn_tile)
    gla_cols = (col_qg, col_kg, col_vg, col_rg)
    mix_scale = jnp.ones((1, n_mix), F32)
    mix_scale = mix_scale.at[:, col_qa:col_ka].set(DIFF_HEAD_DIM ** -0.5 * LOG2E)
    mix_scale = mix_scale.at[:, col_qg:col_kg].set(GLA_DK ** -0.5)
    gate_scale = jnp.ones((1, n_gates), F32)

    for l in range(depth):
        lam_init = 0.8 - 0.6 * math.exp(-0.3 * l)
        mod_l = mod[l]
        w_lr = w_lr_all[l]
        wg_f = jnp.zeros((LANES, gla_k), F32).at[:GLA_LOWRANK].set(w_gate2[l, 0]).astype(BF16)
        wg_b = jnp.zeros((LANES, gla_k), F32).at[GLA_LOWRANK:n_lr].set(w_gate2[l, 1]).astype(BF16)
        bg_f = b_gate2[l, 0].reshape(1, gla_k)
        bg_b = b_gate2[l, 1].reshape(1, gla_k)
        lam_vecs = [v[l].reshape(1, DIFF_HEAD_DIM) for v in (lam_q1, lam_k1, lam_q2, lam_k2)]

        h, lr = _norm_mod(x, norm_mix[l].reshape(1, d), mod_l, 0, 1, seg_of_row_tile, w_lr=w_lr)
        proj = _proj(h, w_in_bf, l, mix_scale, tm, _tile(n_mix, 1024), "in_proj_mix")
        gates = _proj(h, w_gates, l, gate_scale, tm, _tile(n_gates, 1024), "in_proj_gates")

        oa = jax.ShapeDtypeStruct((tp + ts, diff_w), BF16)
        dn = diff_norm[l].reshape(1, 2 * DIFF_HEAD_DIM)
        oa = _diff_attention(proj, oa, 0, bp, sp, bias_tiles, lam_vecs, dn, lam_init, col_qa, col_ka, col_va)
        oa = _diff_attention(proj, oa, tp, bs, ss, bias_tiles, lam_vecs, dn, lam_init, col_qa, col_ka, col_va)

        o_f = jax.ShapeDtypeStruct((tp + ts, gla_w), F32)
        o_f = _gla_pass(proj, lr, wg_f, bg_f, o_f, 0, bp, sp, gla_cols, reverse=False)
        o_f = _gla_pass(proj, lr, wg_f, bg_f, o_f, tp, bs, ss, gla_cols, reverse=False)
        gn = gla_norm[l].reshape(1, GLA_DV)
        ob = jax.ShapeDtypeStruct((tp + ts, gla_w), BF16)
        ob = _gla_pass(proj, lr, wg_b, bg_b, ob, 0, bp, sp, gla_cols, reverse=True, o_fwd=o_f, gla_norm=gn)
        ob = _gla_pass(proj, lr, wg_b, bg_b, ob, tp, bs, ss, gla_cols, reverse=True, o_fwd=o_f, gla_norm=gn)

        merged = _merge(oa, ob, w_bra, w_brb, l, gates, 0, d, tm, _tile(d, 512))
        x = _resid_matmul(merged, w_o, l, x, mod_l, 2, seg_of_row_tile, tm, _tile(d, 512), "out_proj")

        h = _norm_mod(x, norm_ffn[l].reshape(1, d), mod_l, 3, 4, seg_of_row_tile)
        act = _ffn_in(h, w_fi, l, d_ff, tm_ffn, _tile(d_ff, 256))
        x = _resid_matmul(act, w_fo, l, x, mod_l, 5, seg_of_row_tile, _tile(tm, 512), _tile(d, 512), "ffn_out",
                          weights_outer=True)

    gain = norm_final.reshape(1, d)
    return (_final_norm(x, gain, 0, tp).reshape(bp, sp, d), _final_norm(x, gain, tp, ts).reshape(bs, ss, d))
```
